```python
import math
import jax, jax.numpy as jnp
from jax import lax
import numpy as np

D_MODEL = 1024
BATCH = 8
SEQ = 4096
DEPTH = 1

GRID_W = 64
CTX_LEN = 256
EPS = 1e-6

N_HEADS = 8
N_KV_HEADS = 2
HEAD_DIM = 64
AXIS_ROPE_DIM = HEAD_DIM // 2
ROPE_THETA = 10000.0
Q_BLOCK = 128

HY_DIM = D_MODEL // 2
HY_ORDER = 2
HY_SHORT = 3
HY_EMB_BANDS = 16
HY_EMB_DIM = 1 + 2 * HY_EMB_BANDS
HY_FFN = 64
HY_FAST_DECAY = 0.3
HY_SLOW_DECAY = 1.5
HY_TARGET = 1e-2

Q_DIM = N_HEADS * HEAD_DIM
KV_DIM = N_KV_HEADS * HEAD_DIM
HY_COLS = (HY_ORDER + 1) * HY_DIM
GATE_COLS = 2 * D_MODEL
IN_COLS = Q_DIM + 2 * KV_DIM + HY_COLS + GATE_COLS

PEER_HEADS = 8
PEER_N_KEYS = 128
PEER_N_EXPERTS = PEER_N_KEYS * PEER_N_KEYS
PEER_DK = 256
PEER_TOPK = 16
PEER_CHUNK = 128

kernel_name = "hybrid_hyena_gqa_peer_dit_block"


def rmsnorm(x, g):
    xf = x.astype(jnp.float32)
    y = xf * lax.rsqrt(jnp.mean(xf * xf, axis=-1, keepdims=True) + EPS)
    return (y * g.astype(jnp.float32)).astype(x.dtype)


def modulate(h, shift, scale):
    return h * (1.0 + scale) + shift


def axial_rope(rows):
    t = jnp.arange(rows * GRID_W)
    row = (t // GRID_W).astype(jnp.float32)
    col = (t % GRID_W).astype(jnp.float32)
    inv = ROPE_THETA ** (-jnp.arange(0, AXIS_ROPE_DIM, 2, dtype=jnp.float32) / AXIS_ROPE_DIM)
    ar = row[:, None] * inv
    ac = col[:, None] * inv
    ang = jnp.concatenate([ar, ar, ac, ac], axis=-1)
    return jnp.cos(ang), jnp.sin(ang)


def apply_rope(x, cos, sin):
    xf = x.astype(jnp.float32)
    half = AXIS_ROPE_DIM // 2

    def rot(a):
        return jnp.concatenate([-a[..., half:], a[..., :half]], axis=-1)

    xr = jnp.concatenate([rot(xf[..., :AXIS_ROPE_DIM]), rot(xf[..., AXIS_ROPE_DIM:])], axis=-1)
    return (xf * cos[:, None] + xr * sin[:, None]).astype(x.dtype)


def gqa_blocks(q, k, v):
    B, L, _, _ = q.shape
    G = N_HEADS // N_KV_HEADS
    nb = L // Q_BLOCK
    qb = jnp.moveaxis(q.reshape(B, nb, Q_BLOCK, N_KV_HEADS, G, HEAD_DIM), 1, 0)
    scale = HEAD_DIM ** -0.5

    def one_block(qblk):
        s = jnp.einsum('bqkgd,btkd->bkgqt', qblk, k) * scale
        p = jax.nn.softmax(s.astype(jnp.float32), axis=-1).astype(v.dtype)
        return jnp.einsum('bkgqt,btkd->bqkgd', p, v)

    o = lax.map(one_block, qb)
    return jnp.moveaxis(o, 0, 1).reshape(B, L, Q_DIM)


def short_conv(z, w, b):
    zp = jnp.pad(z, ((0, 0), (1, 1), (0, 0)))
    return zp[:, :-2] * w[0] + zp[:, 1:-1] * w[1] + zp[:, 2:] * w[2] + b


def implicit_filters(L, w1, b1, w2, b2, w3, b3, w4, freq):
    t01 = jnp.linspace(0.0, 1.0, L, dtype=jnp.float32)[:, None]
    w = 2.0 * math.pi * jnp.arange(L, dtype=jnp.float32)[:, None] / L
    f = jnp.linspace(1e-4, HY_EMB_BANDS - 1, HY_EMB_BANDS, dtype=jnp.float32)[None]
    z = jnp.concatenate([t01, jnp.cos(f * w), -jnp.sin(f * w)], axis=-1)
    act = lambda a: jnp.sin(freq * a)
    hmid = act(z @ w1 + b1)
    hmid = act(hmid @ w2 + b2)
    hmid = act(hmid @ w3 + b3)
    h = (hmid @ w4).astype(jnp.float32).reshape(L, HY_ORDER, 2, HY_DIM)
    max_decay = math.log(HY_TARGET) / HY_FAST_DECAY
    min_decay = math.log(HY_TARGET) / HY_SLOW_DECAY
    deltas = jnp.abs(jnp.linspace(min_decay, max_decay, HY_DIM, dtype=jnp.float32))
    h = h * jnp.exp(-t01[:, :, None, None] * deltas)
    fwd = h[:, :, 0]
    bwd = h[:, :, 1]
    k = jnp.concatenate([fwd, jnp.zeros_like(fwd[:1]), bwd[:0:-1]], axis=0)
    k = k / (jnp.sum(jnp.abs(k), axis=0, keepdims=True) + EPS)
    return jnp.fft.rfft(k, axis=0)


def fftconv(u, kf, d):
    L = u.shape[1]
    uf = u.astype(jnp.float32)
    U = jnp.fft.rfft(uf, n=2 * L, axis=1)
    y = jnp.fft.irfft(U * kf[None], n=2 * L, axis=1)[:, :L]
    return y + uf * d.astype(jnp.float32)


def hyena(z_hy, conv_w, conv_b, kf, skip):
    zs = short_conv(z_hy, conv_w, conv_b)
    v, x1, x2 = jnp.split(zs, 3, axis=-1)
    y = x1.astype(jnp.float32) * fftconv(v, kf[:, 0], skip[0])
    y = x2.astype(jnp.float32) * fftconv(y, kf[:, 1], skip[1])
    return y.astype(z_hy.dtype)


def mix(h, p, rope, kv_ext):
    B, L, _ = h.shape
    z = h @ p["w_in"]
    q, k, v, z_hy, gates = jnp.split(
        z, [Q_DIM, Q_DIM + KV_DIM, Q_DIM + 2 * KV_DIM, Q_DIM + 2 * KV_DIM + HY_COLS], axis=-1)
    q = rmsnorm(q.reshape(B, L, N_HEADS, HEAD_DIM), p["q_norm_g"])
    k = rmsnorm(k.reshape(B, L, N_KV_HEADS, HEAD_DIM), p["k_norm_g"])
    v = v.reshape(B, L, N_KV_HEADS, HEAD_DIM)
    if rope is not None:
        q = apply_rope(q, rope[0], rope[1])
        k = apply_rope(k, rope[0], rope[1])
    k_self, v_self = k, v
    if kv_ext is not None:
        k = jnp.concatenate([k, kv_ext[0]], axis=1)
        v = jnp.concatenate([v, kv_ext[1]], axis=1)
    attn = gqa_blocks(q, k, v)
    kf = implicit_filters(L, p["hf_w1"], p["hf_b1"], p["hf_w2"], p["hf_b2"],
                          p["hf_w3"], p["hf_b3"], p["hf_w4"], p["hf_freq"])
    hy = hyena(z_hy, p["hy_conv_w"], p["hy_conv_b"], kf, p["hy_skip"])
    g_attn, g_hy = jnp.split(gates, 2, axis=-1)
    merged = (jax.nn.sigmoid(g_attn) * (attn @ p["w_attn_out"])
              + jax.nn.sigmoid(g_hy) * (hy @ p["w_hy_out"]))
    return merged @ p["w_out"], k_self, v_self


def context_kv(hc, p):
    B, L, _ = hc.shape
    z = hc @ p["w_in"][:, Q_DIM:Q_DIM + 2 * KV_DIM]
    k, v = jnp.split(z, 2, axis=-1)
    k = rmsnorm(k.reshape(B, L, N_KV_HEADS, HEAD_DIM), p["k_norm_g"])
    return k, v.reshape(B, L, N_KV_HEADS, HEAD_DIM)


def peer(h, wq, keys1, keys2, u_tab, v_tab):
    B, L, D = h.shape
    xt = h.reshape(B * L // PEER_CHUNK, PEER_CHUNK, D)
    half = PEER_DK // 2

    def chunk(xc):
        q = (xc @ wq).reshape(PEER_CHUNK, PEER_HEADS, 2, half)
        s1 = jnp.einsum('chd,hnd->chn', q[:, :, 0], keys1).astype(jnp.float32)
        s2 = jnp.einsum('chd,hnd->chn', q[:, :, 1], keys2).astype(jnp.float32)
        v1, i1 = lax.top_k(s1, PEER_TOPK)
        v2, i2 = lax.top_k(s2, PEER_TOPK)
        cand = (v1[..., :, None] + v2[..., None, :]).reshape(PEER_CHUNK, PEER_HEADS, PEER_TOPK * PEER_TOPK)
        cidx = (i1[..., :, None] * PEER_N_KEYS + i2[..., None, :]).reshape(PEER_CHUNK, PEER_HEADS, PEER_TOPK * PEER_TOPK)
        best, pos = lax.top_k(cand, PEER_TOPK)
        eidx = jnp.take_along_axis(cidx, pos, axis=-1)
        g = jax.nn.softmax(best, axis=-1)
        a = jnp.einsum('chkd,cd->chk', u_tab[eidx], xc)
        wgt = (jax.nn.gelu(a.astype(jnp.float32)) * g).astype(xc.dtype)
        return jnp.einsum('chk,chkd->cd', wgt, v_tab[eidx])

    return lax.map(chunk, xt).reshape(B, L, D)


def setup_inputs(seed: int = 0) -> dict:
    key = jax.random.key(seed)
    ks = iter(jax.random.split(key, 40))

    def nrm(shape, scale):
        return jax.random.normal(next(ks), shape, jnp.float32) * scale

    L = DEPTH
    return {
        "x": nrm((BATCH, SEQ, D_MODEL), 1.0),
        "c": nrm((BATCH, D_MODEL), 1.0),
        "ctx": nrm((BATCH, CTX_LEN, D_MODEL), 1.0),
        "c_ctx": nrm((D_MODEL,), 1.0),
        "ada_w": nrm((L, D_MODEL, 6 * D_MODEL), D_MODEL ** -0.5),
        "ada_b": nrm((L, 6 * D_MODEL), 0.02),
        "norm_mix_g": 1.0 + nrm((L, D_MODEL), 0.02),
        "norm_ffn_g": 1.0 + nrm((L, D_MODEL), 0.02),
        "w_in": nrm((L, D_MODEL, IN_COLS), D_MODEL ** -0.5),
        "q_norm_g": 1.0 + nrm((L, HEAD_DIM), 0.02),
        "k_norm_g": 1.0 + nrm((L, HEAD_DIM), 0.02),
        "hy_conv_w": nrm((L, HY_SHORT, HY_COLS), HY_SHORT ** -0.5),
        "hy_conv_b": nrm((L, HY_COLS), 0.02),
        "hf_w1": nrm((L, HY_EMB_DIM, HY_FFN), HY_EMB_DIM ** -0.5),
        "hf_b1": nrm((L, HY_FFN), 0.02),
        "hf_w2": nrm((L, HY_FFN, HY_FFN), HY_FFN ** -0.5),
        "hf_b2": nrm((L, HY_FFN), 0.02),
        "hf_w3": nrm((L, HY_FFN, HY_FFN), HY_FFN ** -0.5),
        "hf_b3": nrm((L, HY_FFN), 0.02),
        "hf_w4": nrm((L, HY_FFN, HY_ORDER * 2 * HY_DIM), HY_FFN ** -0.5),
        "hf_freq": 1.0 + nrm((L, HY_FFN), 0.02),
        "hy_skip": nrm((L, HY_ORDER, HY_DIM), 0.5),
        "w_attn_out": nrm((L, Q_DIM, D_MODEL), Q_DIM ** -0.5),
        "w_hy_out": nrm((L, HY_DIM, D_MODEL), HY_DIM ** -0.5),
        "w_out": nrm((L, D_MODEL, D_MODEL), D_MODEL ** -0.5),
        "peer_wq": nrm((L, D_MODEL, PEER_HEADS * PEER_DK), D_MODEL ** -0.5),
        "peer_keys1": nrm((L, PEER_HEADS, PEER_N_KEYS, PEER_DK // 2), (PEER_DK // 2) ** -0.5),
        "peer_keys2": nrm((L, PEER_HEADS, PEER_N_KEYS, PEER_DK // 2), (PEER_DK // 2) ** -0.5),
        "peer_u": nrm((L, PEER_N_EXPERTS, D_MODEL), D_MODEL ** -0.5),
        "peer_v": nrm((L, PEER_N_EXPERTS, D_MODEL), PEER_HEADS ** -0.5),
        "final_norm_g": 1.0 + nrm((D_MODEL,), 0.02),
    }


def reference(x, c, ctx, c_ctx, ada_w, ada_b, norm_mix_g, norm_ffn_g, w_in, q_norm_g, k_norm_g,
              hy_conv_w, hy_conv_b, hf_w1, hf_b1, hf_w2, hf_b2, hf_w3, hf_b3, hf_w4, hf_freq, hy_skip,
              w_attn_out, w_hy_out, w_out, peer_wq, peer_keys1, peer_keys2, peer_u, peer_v, final_norm_g):
    ROWS = x.shape[1] // GRID_W
    rope = axial_rope(ROWS)
    sc = jax.nn.silu(c)
    scc = jax.nn.silu(c_ctx)
    for i in range(DEPTH):
        p = dict(w_in=w_in[i], q_norm_g=q_norm_g[i], k_norm_g=k_norm_g[i],
                 hy_conv_w=hy_conv_w[i], hy_conv_b=hy_conv_b[i],
                 hf_w1=hf_w1[i], hf_b1=hf_b1[i], hf_w2=hf_w2[i], hf_b2=hf_b2[i],
                 hf_w3=hf_w3[i], hf_b3=hf_b3[i], hf_w4=hf_w4[i], hf_freq=hf_freq[i],
                 hy_skip=hy_skip[i], w_attn_out=w_attn_out[i], w_hy_out=w_hy_out[i], w_out=w_out[i])
        mod = (sc @ ada_w[i] + ada_b[i])[:, None, :]
        mod_c = scc @ ada_w[i] + ada_b[i]
        sh1, sc1, g1, sh2, sc2, g2 = jnp.split(mod, 6, axis=-1)
        csh1, csc1, cg1, csh2, csc2, cg2 = jnp.split(mod_c, 6, axis=-1)

        h = modulate(rmsnorm(x, norm_mix_g[i]), sh1, sc1)
        hc = modulate(rmsnorm(ctx, norm_mix_g[i]), csh1, csc1)
        if i < DEPTH - 1:
            out_c, kc, vc = mix(hc, p, None, None)
        else:
            kc, vc = context_kv(hc, p)
        out, _, _ = mix(h, p, rope, (kc, vc))
        x = x + g1 * out
        x = x + g2 * peer(modulate(rmsnorm(x, norm_ffn_g[i]), sh2, sc2),
                          peer_wq[i], peer_keys1[i], peer_keys2[i], peer_u[i], peer_v[i])
        if i < DEPTH - 1:
            ctx = ctx + cg1 * out_c
            ctx = ctx + cg2 * peer(modulate(rmsnorm(ctx, norm_ffn_g[i]), csh2, csc2),
                                   peer_wq[i], peer_keys1[i], peer_keys2[i], peer_u[i], peer_v[i])
    return rmsnorm(x, final_norm_g)
```

```python
import functools
import math

import numpy as np
import jax
import jax.numpy as jnp
from jax import lax
from jax.experimental import pallas as pl
from jax.experimental.pallas import tpu as pltpu

F32 = jnp.float32
BF16 = jnp.bfloat16

D_MODEL = 1024
GRID_W = 64
EPS = 1e-6
N_HEADS = 8
N_KV_HEADS = 2
HEAD_DIM = 64
AXIS_ROPE_DIM = HEAD_DIM // 2
ROPE_THETA = 10000.0
HY_DIM = D_MODEL // 2
HY_ORDER = 2
HY_EMB_BANDS = 16
HY_FFN = 64
HY_FAST_DECAY = 0.3
HY_SLOW_DECAY = 1.5
HY_TARGET = 1e-2
Q_DIM = N_HEADS * HEAD_DIM
KV_DIM = N_KV_HEADS * HEAD_DIM
HY_COLS = (HY_ORDER + 1) * HY_DIM
GATE_COLS = 2 * D_MODEL
PEER_HEADS = 8
PEER_N_KEYS = 128
PEER_DK = 256
PEER_TOPK = 16

LANES = 128
SEQ = 4096
FFT_N = 2 * SEQ
FFT_N1 = 64
FFT_N2 = 128
FFT_NH = FFT_N1 // 2
S_PITCH = FFT_N2 + 8
VMEM_LIMIT = 56 * 1024 * 1024


def _cp(sem, vmem=None):
    return pltpu.CompilerParams(dimension_semantics=sem, vmem_limit_bytes=vmem or VMEM_LIMIT)


def _blk(c):
    return np.block([[c.real, -c.imag], [c.imag, c.real]])


@functools.lru_cache(maxsize=None)
def _dft_tables():
    n2 = np.arange(FFT_N2)
    k1 = np.arange(FFT_N1)
    n1 = np.arange(FFT_N1)
    w64 = np.exp(-2j * np.pi * np.outer(k1, n1) / FFT_N1)
    tw = np.exp(-2j * np.pi * np.outer(n2, k1) / FFT_N)
    full = w64[None, :, :] * tw[:, :, None]
    fa = np.stack([_blk(full[i, :, :FFT_NH]) for i in range(FFT_N2)])
    ga = np.stack([_blk(np.conj(full[i, :, :FFT_NH]).T / FFT_N) for i in range(FFT_N2)])
    f2 = np.exp(-2j * np.pi * np.outer(np.arange(FFT_N2), np.arange(FFT_N2)) / FFT_N2)
    f2b = _blk(f2)
    f2ib = _blk(np.conj(f2))
    ff = np.zeros((FFT_N2, 2 * FFT_N1, 2 * FFT_NH))
    for i in range(FFT_N2):
        cols = np.zeros((FFT_N1, 2 * FFT_NH), dtype=np.complex128)
        cols[:, :FFT_NH] = full[i, :, :FFT_NH]
        r = np.arange(FFT_NH)
        if i == 0:
            cols[:, FFT_NH + 1:] = full[i][:, FFT_N1 - r[1:]]
        else:
            cols[:, FFT_NH:] = full[i][:, FFT_N1 - 1 - r]
        ff[i] = np.concatenate([cols.real, cols.imag], axis=0)
    return fa, ga, f2b, f2ib, ff


@functools.lru_cache(maxsize=None)
def _perm_rows():
    n2 = np.arange(FFT_N2)[:, None]
    n1 = np.arange(FFT_NH)[None, :]
    return (FFT_N2 * n1 + n2).reshape(-1)


@functools.lru_cache(maxsize=None)
def _filter_consts():
    L = SEQ
    t01 = np.linspace(0.0, 1.0, L)[:, None]
    w = 2.0 * np.pi * np.arange(L)[:, None] / L
    f = np.linspace(1e-4, HY_EMB_BANDS - 1, HY_EMB_BANDS)[None]
    z = np.concatenate([t01, np.cos(f * w), -np.sin(f * w)], axis=-1)
    zp = np.zeros((L, LANES))
    zp[:, :z.shape[1]] = z
    max_decay = math.log(HY_TARGET) / HY_FAST_DECAY
    min_decay = math.log(HY_TARGET) / HY_SLOW_DECAY
    deltas = np.abs(np.linspace(min_decay, max_decay, HY_DIM))
    decay = np.exp(-t01 * deltas[None])
    p = _perm_rows()
    return zp[p].astype(np.float32), decay[p].astype(np.float32)


@functools.lru_cache(maxsize=None)
def _rope_tables():
    t = np.arange(SEQ)
    row = (t // GRID_W).astype(np.float64)
    col = (t % GRID_W).astype(np.float64)
    inv = ROPE_THETA ** (-np.arange(0, AXIS_ROPE_DIM, 2, dtype=np.float64) / AXIS_ROPE_DIM)
    ar = row[:, None] * inv
    ac = col[:, None] * inv
    ang = np.concatenate([ar, ar, ac, ac], axis=-1)
    ang2 = np.concatenate([ang, ang], axis=-1)
    return np.cos(ang2).astype(np.float32), np.sin(ang2).astype(np.float32)


def _group_sum_matrix(width):
    g = np.arange(width) // HEAD_DIM
    return (g[:, None] == g[None, :]).astype(np.float32)


def _mod_kernel(c_ref, w_ref, b_ref, o_ref):
    c = c_ref[...]
    sc = c * jax.nn.sigmoid(c)
    o_ref[...] = jnp.dot(sc, w_ref[...], precision=lax.Precision.HIGHEST,
                         preferred_element_type=F32) + b_ref[...]


def _modulation(c16, ada_w, ada_b):
    n = ada_w.shape[1]
    tn = 1536
    return pl.pallas_call(
        _mod_kernel,
        out_shape=jax.ShapeDtypeStruct((16, n), F32),
        grid=(n // tn,),
        in_specs=[pl.BlockSpec((16, D_MODEL), lambda j: (0, 0)),
                  pl.BlockSpec((D_MODEL, tn), lambda j: (0, j)),
                  pl.BlockSpec((1, tn), lambda j: (0, j))],
        out_specs=pl.BlockSpec((16, tn), lambda j: (0, j)),
        compiler_params=_cp(("arbitrary",)),
        name="mod",
    )(c16, ada_w, ada_b.reshape(1, n))


def _norm_mod(x, g, shift, scale):
    ms = jnp.mean(x * x, axis=-1, keepdims=True)
    y = x * lax.rsqrt(ms + EPS) * g
    return y * (1.0 + scale) + shift


def _head_norm(z, gsum_ref, gain):
    zz = z * z
    hi = zz.astype(BF16)
    lo = (zz - hi.astype(F32)).astype(BF16)
    gs = gsum_ref[...]
    ssum = (jnp.dot(hi, gs, preferred_element_type=F32) + jnp.dot(lo, gs, preferred_element_type=F32))
    return z * lax.rsqrt(ssum * (1.0 / HEAD_DIM) + EPS) * gain


def _rope(z, cos, sin):
    outs = []
    lane = lax.broadcasted_iota(jnp.int32, (z.shape[0], LANES), 1)
    first = (lane % AXIS_ROPE_DIM) < (AXIS_ROPE_DIM // 2)
    for t in range(z.shape[1] // LANES):
        zt = z[:, t * LANES:(t + 1) * LANES]
        up = pltpu.roll(zt, LANES - AXIS_ROPE_DIM // 2, axis=1)
        dn = pltpu.roll(zt, AXIS_ROPE_DIM // 2, axis=1)
        rot = jnp.where(first, -up, dn)
        outs.append(zt * cos + rot * sin)
    return outs[0] if len(outs) == 1 else jnp.concatenate(outs, axis=1)


def _pad_pairs(z):
    lane = lax.broadcasted_iota(jnp.int32, z.shape, 1)
    lo = jnp.where(lane < HEAD_DIM, z, 0.0)
    hi = jnp.where(lane >= HEAD_DIM, z, 0.0)
    return lo, pltpu.roll(lo, HEAD_DIM, axis=1), pltpu.roll(hi, HEAD_DIM, axis=1), hi


def _inproj_kernel(x_ref, sh_ref, sc_ref, g_ref, wq_ref, wk_ref, wv_ref, wh_ref, wg_ref,
                   gq_ref, gk_ref, sq_ref, sk_ref, cos_ref, sin_ref,
                   q_ref, k_ref, v_ref, zh_ref, gate_ref):
    h = _norm_mod(x_ref[0], g_ref[...], sh_ref[0], sc_ref[0]).astype(BF16)
    cos = cos_ref[...]
    sin = sin_ref[...]
    q = jnp.dot(h, wq_ref[...], preferred_element_type=F32)
    q = _rope(_head_norm(q, sq_ref, gq_ref[...]), cos, sin)
    q_ref[0] = (q * (HEAD_DIM ** -0.5)).astype(BF16)
    k = jnp.dot(h, wk_ref[...], preferred_element_type=F32)
    k = _rope(_head_norm(k, sk_ref, gk_ref[...]), cos, sin)
    for i, part in enumerate(_pad_pairs(k)):
        k_ref[0, i] = part.astype(BF16)
    v = jnp.dot(h, wv_ref[...], preferred_element_type=F32)
    for i, part in enumerate(_pad_pairs(v)):
        v_ref[0, i] = part.astype(BF16)
    zh_ref[0] = jnp.dot(h, wh_ref[...], preferred_element_type=F32)
    gate_ref[0] = jnp.dot(h, wg_ref[...], preferred_element_type=F32).astype(BF16)


def _inproj(x, sh, sc, g, wq, wk, wv, wh, wg, gq, gk, cos, sin, tm=512):
    B, L, D = x.shape
    full = lambda a: pl.BlockSpec(a.shape, lambda b, i: (0,) * a.ndim)
    row = pl.BlockSpec((1, 1, D), lambda b, i: (b, 0, 0))
    sq = jnp.asarray(_group_sum_matrix(Q_DIM), BF16)
    sk = jnp.asarray(_group_sum_matrix(KV_DIM), BF16)
    return pl.pallas_call(
        _inproj_kernel,
        out_shape=(jax.ShapeDtypeStruct((B, L, Q_DIM), BF16),
                   jax.ShapeDtypeStruct((B, 4, L, LANES), BF16),
                   jax.ShapeDtypeStruct((B, 4, L, LANES), BF16),
                   jax.ShapeDtypeStruct((B, L, HY_COLS), F32),
                   jax.ShapeDtypeStruct((B, L, GATE_COLS), BF16)),
        grid=(B, L // tm),
        in_specs=[pl.BlockSpec((1, tm, D), lambda b, i: (b, i, 0)), row, row, full(g),
                  full(wq), full(wk), full(wv), full(wh), full(wg), full(gq), full(gk), full(sq), full(sk),
                  pl.BlockSpec((tm, LANES), lambda b, i: (i, 0)),
                  pl.BlockSpec((tm, LANES), lambda b, i: (i, 0))],
        out_specs=(pl.BlockSpec((1, tm, Q_DIM), lambda b, i: (b, i, 0)),
                   pl.BlockSpec((1, 4, tm, LANES), lambda b, i: (b, 0, i, 0)),
                   pl.BlockSpec((1, 4, tm, LANES), lambda b, i: (b, 0, i, 0)),
                   pl.BlockSpec((1, tm, HY_COLS), lambda b, i: (b, i, 0)),
                   pl.BlockSpec((1, tm, GATE_COLS), lambda b, i: (b, i, 0))),
        compiler_params=_cp(("arbitrary", "arbitrary")),
        name="inproj",
    )(x, sh, sc, g, wq, wk, wv, wh, wg, gq, gk, sq, sk, cos, sin)


def _ctxkv_kernel(x_ref, sh_ref, sc_ref, g_ref, wk_ref, wv_ref, gk_ref, sk_ref, k_ref, v_ref):
    h = _norm_mod(x_ref[0], g_ref[...], sh_ref[...], sc_ref[...]).astype(BF16)
    k = _head_norm(jnp.dot(h, wk_ref[...], preferred_element_type=F32), sk_ref, gk_ref[...])
    for i, part in enumerate(_pad_pairs(k)):
        k_ref[0, i] = part.astype(BF16)
    v = jnp.dot(h, wv_ref[...], preferred_element_type=F32)
    for i, part in enumerate(_pad_pairs(v)):
        v_ref[0, i] = part.astype(BF16)


def _ctxkv(ctx, sh, sc, g, wk, wv, gk):
    B, T, D = ctx.shape
    full = lambda a: pl.BlockSpec(a.shape, lambda b: (0,) * a.ndim)
    sk = jnp.asarray(_group_sum_matrix(KV_DIM), BF16)
    return pl.pallas_call(
        _ctxkv_kernel,
        out_shape=(jax.ShapeDtypeStruct((B, 4, T, LANES), BF16),
                   jax.ShapeDtypeStruct((B, 4, T, LANES), BF16)),
        grid=(B,),
        in_specs=[pl.BlockSpec((1, T, D), lambda b: (b, 0, 0)), full(sh), full(sc), full(g),
                  full(wk), full(wv), full(gk), full(sk)],
        out_specs=(pl.BlockSpec((1, 4, T, LANES), lambda b: (b, 0, 0, 0)),
                   pl.BlockSpec((1, 4, T, LANES), lambda b: (b, 0, 0, 0))),
        compiler_params=_cp(("arbitrary",)),
        name="ctxkv",
    )(ctx, sh, sc, g, wk, wv, gk, sk)


def _attn_kernel(q_ref, ks_ref, kc_ref, vs_ref, vc_ref, o_ref):
    nt = (((1,), (1,)), ((), ()))
    for pair in range(2):
        q2 = q_ref[0, :, pair * LANES:(pair + 1) * LANES]
        acc = None
        for hh in range(2):
            s_s = lax.dot_general(q2, ks_ref[0, hh], nt, preferred_element_type=F32)
            s_c = lax.dot_general(q2, kc_ref[0, hh], nt, preferred_element_type=F32)
            m = jnp.maximum(jnp.max(s_s, axis=-1, keepdims=True), jnp.max(s_c, axis=-1, keepdims=True))
            p_s = jnp.exp(s_s - m)
            p_c = jnp.exp(s_c - m)
            l = jnp.sum(p_s, axis=-1, keepdims=True) + jnp.sum(p_c, axis=-1, keepdims=True)
            o = (jnp.dot(p_s.astype(BF16), vs_ref[0, hh], preferred_element_type=F32)
                 + jnp.dot(p_c.astype(BF16), vc_ref[0, hh], preferred_element_type=F32)) / l
            acc = o if acc is None else acc + o
        o_ref[0, :, pair * LANES:(pair + 1) * LANES] = acc.astype(BF16)


def _attention(q, ks, kc, vs, vc, tq=256):
    B, L, _ = q.shape
    T = kc.shape[2]
    kv_self = pl.BlockSpec((1, 2, L, LANES), lambda b, h, i: (b, h, 0, 0))
    kv_ctx = pl.BlockSpec((1, 2, T, LANES), lambda b, h, i: (b, h, 0, 0))
    qo = pl.BlockSpec((1, tq, 2 * LANES), lambda b, h, i: (b, i, h))
    return pl.pallas_call(
        _attn_kernel,
        out_shape=jax.ShapeDtypeStruct((B, L, Q_DIM), BF16),
        grid=(B, N_KV_HEADS, L // tq),
        in_specs=[qo, kv_self, kv_ctx, kv_self, kv_ctx],
        out_specs=qo,
        compiler_params=_cp(("arbitrary", "arbitrary", "arbitrary")),
        name="attn",
    )(q, ks, kc, vs, vc)


def _strided(n2):
    return pl.ds(n2, FFT_N2, stride=S_PITCH)


def _filt_kernel(z_ref, dec_ref, w1_ref, b1_ref, w2_ref, b2_ref, w3_ref, b3_ref, w4_ref, fr_ref,
                 ff_ref, f2_ref, kf_ref, hf_ref, hb_ref, s_ref):
    hp = lax.Precision.HIGHEST
    fr = fr_ref[...]
    a = jnp.sin(fr * (jnp.dot(z_ref[...], w1_ref[...], precision=hp, preferred_element_type=F32) + b1_ref[...]))
    a = jnp.sin(fr * (jnp.dot(a, w2_ref[...], precision=hp, preferred_element_type=F32) + b2_ref[...]))
    a = jnp.sin(fr * (jnp.dot(a, w3_ref[...], precision=hp, preferred_element_type=F32) + b3_ref[...]))
    h = jnp.dot(a, w4_ref[0], precision=hp, preferred_element_type=F32)
    dec = dec_ref[...]
    fwd = h[:, :LANES] * dec
    bwd = h[:, LANES:] * dec
    norm = (jnp.sum(jnp.abs(fwd), axis=0, keepdims=True) + jnp.sum(jnp.abs(bwd), axis=0, keepdims=True)
            - jnp.abs(bwd[0:1, :]) + EPS)
    inv = 1.0 / norm
    hf_ref[...] = fwd * inv
    hb_ref[...] = bwd * inv

    def outer(n2, carry):
        nb = jnp.where(n2 == 0, 0, FFT_N2 - n2)
        rf = hf_ref[pl.ds(pl.multiple_of(n2 * FFT_NH, FFT_NH), FFT_NH), :]
        rb = hb_ref[pl.ds(pl.multiple_of(nb * FFT_NH, FFT_NH), FFT_NH), :]
        rhs = jnp.concatenate([rf, rb], axis=0).astype(BF16)
        s_ref[_strided(n2), :] = jnp.dot(ff_ref[n2], rhs, preferred_element_type=F32)
        return carry

    lax.fori_loop(0, FFT_N2, outer, 0)

    def inner(k1, carry):
        re = s_ref[pl.ds(pl.multiple_of(k1 * S_PITCH, 8), FFT_N2), :]
        im = s_ref[pl.ds(pl.multiple_of((FFT_N1 + k1) * S_PITCH, 8), FFT_N2), :]
        rhs = jnp.concatenate([re, im], axis=0).astype(BF16)
        kf_ref[0, k1] = jnp.dot(f2_ref[...], rhs, preferred_element_type=F32).astype(BF16)
        return carry

    lax.fori_loop(0, FFT_N1, inner, 0)


def _filters(hf_w1, hf_b1, hf_w2, hf_b2, hf_w3, hf_b3, hf_w4, hf_freq):
    zp, decay = _filter_consts()
    _, _, f2b, _, ff = _dft_tables()
    w1 = jnp.zeros((LANES, HY_FFN), F32).at[:hf_w1.shape[0]].set(hf_w1)
    nct = HY_DIM // LANES
    w4 = hf_w4.reshape(HY_FFN, HY_ORDER, 2, nct, LANES).transpose(1, 3, 0, 2, 4)
    w4 = w4.reshape(HY_ORDER * nct, HY_FFN, 2 * LANES)
    full = lambda a: pl.BlockSpec(a.shape, lambda o, c: (0,) * a.ndim)
    vec = lambda v: v.reshape(1, HY_FFN)
    args = (jnp.asarray(zp), jnp.asarray(decay), w1, vec(hf_b1), hf_w2, vec(hf_b2), hf_w3, vec(hf_b3), w4,
            vec(hf_freq), jnp.asarray(ff, BF16), jnp.asarray(f2b, BF16))
    in_specs = [full(args[0]), pl.BlockSpec((SEQ, LANES), lambda o, c: (0, c))]
    in_specs += [full(a) for a in args[2:8]]
    in_specs += [pl.BlockSpec((1, HY_FFN, 2 * LANES), lambda o, c: (o * nct + c, 0, 0))]
    in_specs += [full(a) for a in args[9:]]
    return pl.pallas_call(
        _filt_kernel,
        out_shape=jax.ShapeDtypeStruct((HY_ORDER, FFT_N1, 2 * FFT_N2, HY_DIM), BF16),
        grid=(HY_ORDER, nct),
        in_specs=in_specs,
        out_specs=pl.BlockSpec((1, FFT_N1, 2 * FFT_N2, LANES), lambda o, c: (o, 0, 0, c)),
        scratch_shapes=[pltpu.VMEM((SEQ, LANES), F32), pltpu.VMEM((SEQ, LANES), F32),
                        pltpu.VMEM((2 * FFT_N1 * S_PITCH, LANES), F32)],
        compiler_params=_cp(("arbitrary", "arbitrary")),
        name="filt",
    )(*args)


def _conv_kernel(sc_u, u_ref, g_ref, wu_ref, bu_ref, wg_ref, bg_ref, skip_ref, kf_ref,
                 fa_ref, ga_ref, f2_ref, f2i_ref, o_ref, s_ref):
    row = lax.broadcasted_iota(jnp.int32, (FFT_NH, LANES), 0)

    def blk(ref, n2):
        return [ref[b, pl.ds(n2, FFT_NH, stride=FFT_N2), :] for b in range(2)]

    def down(x):
        return jnp.where(row == 0, 0.0, pltpu.roll(x, 1, axis=0))

    def up(x):
        return jnp.where(row == FFT_NH - 1, 0.0, pltpu.roll(x, FFT_NH - 1, axis=0))

    def cat(pair):
        return jnp.concatenate(pair, axis=0)

    def sconv(w_ref, b_ref, prev, cur, nxt):
        return prev * w_ref[0] + cur * w_ref[1] + nxt * w_ref[2] + b_ref[...]

    def first(ref):
        return cat([down(x) for x in blk(ref, FFT_N2 - 1)]), cat(blk(ref, 0))

    def last(ref):
        return cat([up(x) for x in blk(ref, 0)])

    def fwd_store(n2, ublk):
        s_ref[_strided(n2), :] = jnp.dot(fa_ref[n2], ublk.astype(BF16), preferred_element_type=F32)

    if sc_u:
        def fwd_body(n2, carry):
            prev, cur = carry
            nxt = cat(blk(u_ref, n2 + 1))
            fwd_store(n2, sconv(wu_ref, bu_ref, prev, cur, nxt))
            return cur, nxt

        prev, cur = lax.fori_loop(0, FFT_N2 - 1, fwd_body, first(u_ref))
        fwd_store(FFT_N2 - 1, sconv(wu_ref, bu_ref, prev, cur, last(u_ref)))
    else:
        def fwd_body(n2, carry):
            fwd_store(n2, cat(blk(u_ref, n2)))
            return carry

        lax.fori_loop(0, FFT_N2, fwd_body, 0)

    def mid_body(k1, carry):
        r0 = pl.multiple_of(k1 * S_PITCH, 8)
        r1 = pl.multiple_of((FFT_N1 + k1) * S_PITCH, 8)
        rhs = jnp.concatenate([s_ref[pl.ds(r0, FFT_N2), :], s_ref[pl.ds(r1, FFT_N2), :]], axis=0).astype(BF16)
        x = jnp.dot(f2_ref[...], rhs, preferred_element_type=F32)
        kf = kf_ref[0, k1].astype(F32)
        xr, xi = x[:FFT_N2], x[FFT_N2:]
        kr, ki = kf[:FFT_N2], kf[FFT_N2:]
        y = jnp.concatenate([xr * kr - xi * ki, xr * ki + xi * kr], axis=0).astype(BF16)
        p = jnp.dot(f2i_ref[...], y, preferred_element_type=F32)
        s_ref[pl.ds(r0, FFT_N2), :] = p[:FFT_N2]
        s_ref[pl.ds(r1, FFT_N2), :] = p[FFT_N2:]
        return carry

    lax.fori_loop(0, FFT_N1, mid_body, 0)

    skip = skip_ref[0]

    def out_store(n2, ublk, gblk):
        c = jnp.dot(ga_ref[n2], s_ref[_strided(n2), :].astype(BF16), preferred_element_type=F32)
        y = gblk * (c + skip * ublk)
        o_ref[0, pl.ds(n2, FFT_NH, stride=FFT_N2), :] = y[:FFT_NH]
        o_ref[1, pl.ds(n2, FFT_NH, stride=FFT_N2), :] = y[FFT_NH:]

    if sc_u:
        def out_body(n2, carry):
            pu, cu, pg, cg = carry
            nu = cat(blk(u_ref, n2 + 1))
            ng = cat(blk(g_ref, n2 + 1))
            out_store(n2, sconv(wu_ref, bu_ref, pu, cu, nu), sconv(wg_ref, bg_ref, pg, cg, ng))
            return cu, nu, cg, ng

        pu, cu, pg, cg = lax.fori_loop(0, FFT_N2 - 1, out_body, first(u_ref) + first(g_ref))
        out_store(FFT_N2 - 1, sconv(wu_ref, bu_ref, pu, cu, last(u_ref)),
                  sconv(wg_ref, bg_ref, pg, cg, last(g_ref)))
    else:
        def out_body(n2, carry):
            pg, cg = carry
            ng = cat(blk(g_ref, n2 + 1))
            out_store(n2, cat(blk(u_ref, n2)), sconv(wg_ref, bg_ref, pg, cg, ng))
            return cg, ng

        pg, cg = lax.fori_loop(0, FFT_N2 - 1, out_body, first(g_ref))
        out_store(FFT_N2 - 1, cat(blk(u_ref, FFT_N2 - 1)), sconv(wg_ref, bg_ref, pg, cg, last(g_ref)))


def _longconv(u, u_part, gate, g_part, conv_w, conv_b, skip, kf, order, sc_u):
    B = u.shape[0]
    nct = HY_DIM // LANES
    fa, ga, f2b, f2ib, _ = _dft_tables()
    fa, ga = jnp.asarray(fa, BF16), jnp.asarray(ga, BF16)
    f2b, f2ib = jnp.asarray(f2b, BF16), jnp.asarray(f2ib, BF16)
    w3 = conv_w.reshape(3, 1, HY_COLS)
    b3 = conv_b.reshape(1, HY_COLS)
    skip2 = skip.reshape(HY_ORDER, 1, HY_DIM)
    full = lambda a: pl.BlockSpec(a.shape, lambda c, p: (0,) * a.ndim)
    sig = lambda part: pl.BlockSpec((2, SEQ, LANES), lambda c, p: (p, 0, part * nct + c))
    wspec = lambda part: pl.BlockSpec((3, 1, LANES), lambda c, p: (0, 0, part * nct + c))
    bspec = lambda part: pl.BlockSpec((1, LANES), lambda c, p: (0, part * nct + c))
    wu_part = u_part if sc_u else g_part
    return pl.pallas_call(
        functools.partial(_conv_kernel, sc_u),
        out_shape=jax.ShapeDtypeStruct((B, SEQ, HY_DIM), F32),
        grid=(nct, B // 2),
        in_specs=[sig(u_part), sig(g_part), wspec(wu_part), bspec(wu_part), wspec(g_part), bspec(g_part),
                  pl.BlockSpec((1, 1, LANES), lambda c, p: (order, 0, c)),
                  pl.BlockSpec((1, FFT_N1, 2 * FFT_N2, LANES), lambda c, p: (order, 0, 0, c)),
                  full(fa), full(ga), full(f2b), full(f2ib)],
        out_specs=pl.BlockSpec((2, SEQ, LANES), lambda c, p: (p, 0, c)),
        scratch_shapes=[pltpu.VMEM((2 * FFT_N1 * S_PITCH, LANES), F32)],
        compiler_params=_cp(("arbitrary", "arbitrary")),
        name="conv%d" % order,
    )(u, gate, w3, b3, w3, b3, skip2, kf, fa, ga, f2b, f2ib)


def _merge_kernel(x_ref, at_ref, hy_ref, gt_ref, g1_ref, sh_ref, sc_ref, ng_ref, wa_ref, wh_ref, wo_ref,
                  x1_ref, h2_ref):
    ga = jax.nn.sigmoid(gt_ref[0, :, :D_MODEL].astype(F32))
    gh = jax.nn.sigmoid(gt_ref[0, :, D_MODEL:].astype(F32))
    merged = (ga * jnp.dot(at_ref[0], wa_ref[...], preferred_element_type=F32)
              + gh * jnp.dot(hy_ref[0].astype(BF16), wh_ref[...], preferred_element_type=F32))
    out = jnp.dot(merged.astype(BF16), wo_ref[...], preferred_element_type=F32)
    x1 = x_ref[0] + g1_ref[0] * out
    x1_ref[0] = x1
    h2_ref[0] = _norm_mod(x1, ng_ref[...], sh_ref[0], sc_ref[0]).astype(BF16)


def _merge(x, attn, hy, gates, g1, sh2, sc2, ng, wa, wh, wo, tm=512):
    B, L, D = x.shape
    full = lambda a: pl.BlockSpec(a.shape, lambda b, i: (0,) * a.ndim)
    row = pl.BlockSpec((1, 1, D), lambda b, i: (b, 0, 0))
    tok = lambda w: pl.BlockSpec((1, tm, w), lambda b, i: (b, i, 0))
    return pl.pallas_call(
        _merge_kernel,
        out_shape=(jax.ShapeDtypeStruct((B, L, D), F32), jax.ShapeDtypeStruct((B, L, D), BF16)),
        grid=(B, L // tm),
        in_specs=[tok(D), tok(Q_DIM), tok(HY_DIM), tok(GATE_COLS), row, row, row, full(ng),
                  full(wa), full(wh), full(wo)],
        out_specs=(tok(D), tok(D)),
        compiler_params=_cp(("arbitrary", "arbitrary")),
        name="merge",
    )(x, attn, hy, gates, g1, sh2, sc2, ng, wa, wh, wo)


def _top16_rows(s, idx):
    rows = s.shape[0]
    pos = lax.broadcasted_iota(jnp.int32, s.shape, 0).astype(F32)
    vals, pays = [], []
    for _ in range(PEER_TOPK):
        m = jnp.max(s, axis=0, keepdims=True)
        first = jnp.min(jnp.where(s == m, pos, float(rows)), axis=0, keepdims=True)
        hit = pos == first
        pays.append(jnp.sum(jnp.where(hit, idx, 0.0), axis=0, keepdims=True))
        vals.append(m)
        s = jnp.where(hit, -jnp.inf, s)
    return vals, pays


def _topk_kernel(h_ref, wq_ref, k1_ref, k2_ref, e_ref, g_ref):
    tt = h_ref.shape[0]
    nt = (((1,), (1,)), ((), ()))
    xq = lax.dot_general(wq_ref[...], h_ref[...], nt, preferred_element_type=F32)
    key_id = lax.broadcasted_iota(jnp.int32, (PEER_N_KEYS, tt), 0).astype(F32)
    half = PEER_DK // 2
    for h in range(PEER_HEADS):
        q1 = xq[h * PEER_DK:h * PEER_DK + half].astype(BF16)
        q2 = xq[h * PEER_DK + half:(h + 1) * PEER_DK].astype(BF16)
        s1 = jnp.dot(k1_ref[h], q1, preferred_element_type=F32)
        s2 = jnp.dot(k2_ref[h], q2, preferred_element_type=F32)
        v1, i1 = _top16_rows(s1, key_id)
        v2, i2 = _top16_rows(s2, key_id)
        v1 = jnp.concatenate(v1, axis=0)
        v2 = jnp.concatenate(v2, axis=0)
        i1 = jnp.concatenate(i1, axis=0)
        i2 = jnp.concatenate(i2, axis=0)
        cand = jnp.concatenate([v1[k:k + 1] + v2 for k in range(PEER_TOPK)], axis=0)
        cidx = jnp.concatenate([i1[k:k + 1] * float(PEER_N_KEYS) + i2 for k in range(PEER_TOPK)], axis=0)
        best, eidx = _top16_rows(cand, cidx)
        best = jnp.concatenate(best, axis=0)
        p = jnp.exp(best - best[0:1])
        g_ref[h * PEER_TOPK:(h + 1) * PEER_TOPK, :] = p / jnp.sum(p, axis=0, keepdims=True)
        e_ref[h * PEER_TOPK:(h + 1) * PEER_TOPK, :] = jnp.concatenate(eidx, axis=0)


def _peer_topk(h2, wq_t, keys1, keys2, tt=256):
    n = h2.shape[0]
    full = lambda a: pl.BlockSpec(a.shape, lambda i: (0,) * a.ndim)
    out = jax.ShapeDtypeStruct((PEER_HEADS * PEER_TOPK, n), F32)
    ospec = pl.BlockSpec((PEER_HEADS * PEER_TOPK, tt), lambda i: (0, i))
    return pl.pallas_call(
        _topk_kernel,
        out_shape=(out, out),
        grid=(n // tt,),
        in_specs=[pl.BlockSpec((tt, D_MODEL), lambda i: (i, 0)), full(wq_t), full(keys1), full(keys2)],
        out_specs=(ospec, ospec),
        compiler_params=_cp(("arbitrary",)),
        name="topk",
    )(h2, wq_t, keys1, keys2)


def _gmat_kernel(e_ref, g_ref, o_ref, s_ref):
    tm = e_ref.shape[0]
    sub = 32
    nk = PEER_N_KEYS
    ids = lax.broadcasted_iota(jnp.int32, (sub, nk, nk), 1).astype(F32)
    for c0 in range(0, tm, sub):
        e = e_ref[c0:c0 + sub, :]
        a = jnp.floor(e * (1.0 / nk))
        b = e - a * nk
        g = g_ref[c0:c0 + sub, :]
        at = jnp.where(ids == a[:, None, :], 1.0, 0.0).astype(BF16)
        bt = jnp.where(ids == b[:, None, :], g[:, None, :], 0.0).astype(BF16)
        g3 = jnp.einsum("caj,cbj->cab", at, bt, preferred_element_type=F32)
        for c in range(sub):
            s_ref[pl.ds(c0 + c, nk, stride=S_PITCH), :] = g3[c]
    for a in range(nk):
        o_ref[a] = s_ref[a * S_PITCH:a * S_PITCH + tm, :].astype(BF16)


def _peer_gmat(e, g, tm=128):
    n = e.shape[0]
    return pl.pallas_call(
        _gmat_kernel,
        out_shape=jax.ShapeDtypeStruct((PEER_N_KEYS, n, PEER_N_KEYS), BF16),
        grid=(n // tm,),
        in_specs=[pl.BlockSpec((tm, LANES), lambda i: (i, 0)), pl.BlockSpec((tm, LANES), lambda i: (i, 0))],
        out_specs=pl.BlockSpec((PEER_N_KEYS, tm, PEER_N_KEYS), lambda i: (0, i, 0)),
        scratch_shapes=[pltpu.VMEM((PEER_N_KEYS * S_PITCH, LANES), F32)],
        compiler_params=_cp(("arbitrary",)),
        name="gmat",
    )(e, g)


def _dense_kernel(h_ref, ut_ref, v_ref, gm_ref, x1_ref, g2_ref, fg_ref, o_ref, acc_ref):
    j = pl.program_id(1)

    @pl.when(j == 0)
    def _():
        acc_ref[...] = jnp.zeros_like(acc_ref)

    a = jnp.dot(h_ref[...], ut_ref[...], preferred_element_type=F32)
    na = gm_ref.shape[0]
    w = jnp.concatenate(
        [jax.nn.gelu(a[:, i * LANES:(i + 1) * LANES], approximate=True) * gm_ref[i].astype(F32)
         for i in range(na)], axis=1).astype(BF16)
    acc_ref[...] += jnp.dot(w, v_ref[...], preferred_element_type=F32)

    @pl.when(j == pl.num_programs(1) - 1)
    def _():
        x2 = x1_ref[...] + g2_ref[0] * acc_ref[...]
        ms = jnp.mean(x2 * x2, axis=-1, keepdims=True)
        o_ref[...] = x2 * lax.rsqrt(ms + EPS) * fg_ref[...]


def _peer_dense(h2, u_t, v, gm, x1, g2, fg, seq, tm=512, ne=1024):
    n, d = h2.shape
    n_exp = v.shape[0]
    per_b = seq // tm
    return pl.pallas_call(
        _dense_kernel,
        out_shape=jax.ShapeDtypeStruct((n, d), F32),
        grid=(n // tm, n_exp // ne),
        in_specs=[pl.BlockSpec((tm, d), lambda i, j: (i, 0)),
                  pl.BlockSpec((d, ne), lambda i, j: (0, j)),
                  pl.BlockSpec((ne, d), lambda i, j: (j, 0)),
                  pl.BlockSpec((ne // LANES, tm, LANES), lambda i, j: (j, i, 0)),
                  pl.BlockSpec((tm, d), lambda i, j: (i, 0)),
                  pl.BlockSpec((1, 1, d), lambda i, j: (i // per_b, 0, 0)),
                  pl.BlockSpec((1, d), lambda i, j: (0, 0))],
        out_specs=pl.BlockSpec((tm, d), lambda i, j: (i, 0)),
        scratch_shapes=[pltpu.VMEM((tm, d), F32)],
        compiler_params=_cp(("arbitrary", "arbitrary")),
        name="dense",
    )(h2, u_t, v, gm, x1, g2, fg)


def kernel(x, c, ctx, c_ctx, ada_w, ada_b, norm_mix_g, norm_ffn_g, w_in, q_norm_g, k_norm_g, hy_conv_w, hy_conv_b, hf_w1, hf_b1, hf_w2, hf_b2, hf_w3, hf_b3, hf_w4, hf_freq, hy_skip, w_attn_out, w_hy_out, w_out, peer_wq, peer_keys1, peer_keys2, peer_u, peer_v, final_norm_g):
    B, L, D = x.shape
    assert L == SEQ and D == D_MODEL and B % 2 == 0 and ada_w.shape[0] == 1
    bf = lambda a: a.astype(BF16)

    c16 = jnp.zeros((16, D), F32).at[:B].set(c).at[B].set(c_ctx)
    mod = _modulation(c16, ada_w[0], ada_b[0])
    part = lambda i: mod[:B, i * D:(i + 1) * D].reshape(B, 1, D)
    sh1, sc1, g1, sh2, sc2, g2 = (part(i) for i in range(6))
    csh1, csc1 = mod[B:B + 1, 0:D], mod[B:B + 1, D:2 * D]

    w = w_in[0]
    o_k, o_v, o_h, o_g = Q_DIM, Q_DIM + KV_DIM, Q_DIM + 2 * KV_DIM, Q_DIM + 2 * KV_DIM + HY_COLS
    wq, wk, wv, wh, wg = bf(w[:, :o_k]), bf(w[:, o_k:o_v]), bf(w[:, o_v:o_h]), bf(w[:, o_h:o_g]), bf(w[:, o_g:])
    gq = jnp.tile(q_norm_g[0], N_HEADS).reshape(1, Q_DIM)
    gk = jnp.tile(k_norm_g[0], N_KV_HEADS).reshape(1, KV_DIM)
    ng1 = norm_mix_g[0].reshape(1, D)
    cos, sin = (jnp.asarray(t) for t in _rope_tables())

    q, ks, vs, zhy, gates = _inproj(x, sh1, sc1, ng1, wq, wk, wv, wh, wg, gq, gk, cos, sin)
    kc, vc = _ctxkv(ctx, csh1, csc1, ng1, wk, wv, gk)
    attn = _attention(q, ks, kc, vs, vc)

    kf = _filters(hf_w1[0], hf_b1[0], hf_w2[0], hf_b2[0], hf_w3[0], hf_b3[0], hf_w4[0], hf_freq[0])
    y1 = _longconv(zhy, 0, zhy, 1, hy_conv_w[0], hy_conv_b[0], hy_skip[0], kf, 0, True)
    y2 = _longconv(y1, 0, zhy, 2, hy_conv_w[0], hy_conv_b[0], hy_skip[0], kf, 1, False)

    x1, h2 = _merge(x, attn, y2, gates, g1, sh2, sc2, norm_ffn_g[0].reshape(1, D),
                    bf(w_attn_out[0]), bf(w_hy_out[0]), bf(w_out[0]))

    n = B * L
    h2f = h2.reshape(n, D)
    e_t, g_t = _peer_topk(h2f, bf(peer_wq[0].T), bf(peer_keys1[0]), bf(peer_keys2[0]))
    gm = _peer_gmat(e_t.T, g_t.T)
    out = _peer_dense(h2f, bf(peer_u[0].T), bf(peer_v[0]), gm, x1.reshape(n, D), g2,
                      final_norm_g.reshape(1, D), L)
    return out.reshape(B, L, D)
```

```python
import functools
import math

import numpy as np
import jax
import jax.numpy as jnp
from jax import lax
from jax.experimental import pallas as pl
from jax.experimental.pallas import tpu as pltpu

F32 = jnp.float32
BF16 = jnp.bfloat16

D_MODEL = 1024
GRID_W = 64
EPS = 1e-6
N_HEADS = 8
N_KV_HEADS = 2
HEAD_DIM = 64
AXIS_ROPE_DIM = HEAD_DIM // 2
ROPE_THETA = 10000.0
HY_DIM = D_MODEL // 2
HY_ORDER = 2
HY_EMB_BANDS = 16
HY_FFN = 64
HY_FAST_DECAY = 0.3
HY_SLOW_DECAY = 1.5
HY_TARGET = 1e-2
Q_DIM = N_HEADS * HEAD_DIM
KV_DIM = N_KV_HEADS * HEAD_DIM
HY_COLS = (HY_ORDER + 1) * HY_DIM
GATE_COLS = 2 * D_MODEL
PEER_HEADS = 8
PEER_N_KEYS = 128
PEER_DK = 256
PEER_TOPK = 16

LANES = 128
SEQ = 4096
FFT_N = 2 * SEQ
FFT_N1 = 64
FFT_N2 = 128
FFT_NH = FFT_N1 // 2
S_PITCH = FFT_N2 + 8
VMEM_LIMIT = 56 * 1024 * 1024
ATTN_TK = 2048
DFT_UNROLL = 4
Q_SCALE = HEAD_DIM ** -0.5 * math.log2(math.e)


def _cp(sem, vmem=None):
    return pltpu.CompilerParams(dimension_semantics=sem, vmem_limit_bytes=vmem or VMEM_LIMIT)


def _blk(c):
    return np.block([[c.real, -c.imag], [c.imag, c.real]])


@functools.lru_cache(maxsize=None)
def _dft_tables():
    n2 = np.arange(FFT_N2)
    k1 = np.arange(FFT_N1)
    n1 = np.arange(FFT_N1)
    w64 = np.exp(-2j * np.pi * np.outer(k1, n1) / FFT_N1)
    tw = np.exp(-2j * np.pi * np.outer(n2, k1) / FFT_N)
    full = w64[None, :, :] * tw[:, :, None]
    fa = np.stack([_blk(full[i, :, :FFT_NH]) for i in range(FFT_N2)])
    ga = np.stack([_blk(np.conj(full[i, :, :FFT_NH]).T / FFT_N) for i in range(FFT_N2)])
    f2 = np.exp(-2j * np.pi * np.outer(np.arange(FFT_N2), np.arange(FFT_N2)) / FFT_N2)
    f2b = _blk(f2)
    f2ib = _blk(np.conj(f2))
    ff = np.zeros((FFT_N2, 2 * FFT_N1, 2 * FFT_NH))
    for i in range(FFT_N2):
        cols = np.zeros((FFT_N1, 2 * FFT_NH), dtype=np.complex128)
        cols[:, :FFT_NH] = full[i, :, :FFT_NH]
        r = np.arange(FFT_NH)
        if i == 0:
            cols[:, FFT_NH + 1:] = full[i][:, FFT_N1 - r[1:]]
        else:
            cols[:, FFT_NH:] = full[i][:, FFT_N1 - 1 - r]
        ff[i] = np.concatenate([cols.real, cols.imag], axis=0)
    return fa, ga, f2b, f2ib, ff


@functools.lru_cache(maxsize=None)
def _perm_rows():
    n2 = np.arange(FFT_N2)[:, None]
    n1 = np.arange(FFT_NH)[None, :]
    return (FFT_N2 * n1 + n2).reshape(-1)


@functools.lru_cache(maxsize=None)
def _filter_consts():
    L = SEQ
    t01 = np.linspace(0.0, 1.0, L)[:, None]
    w = 2.0 * np.pi * np.arange(L)[:, None] / L
    f = np.linspace(1e-4, HY_EMB_BANDS - 1, HY_EMB_BANDS)[None]
    z = np.concatenate([t01, np.cos(f * w), -np.sin(f * w)], axis=-1)
    zp = np.zeros((L, LANES))
    zp[:, :z.shape[1]] = z
    max_decay = math.log(HY_TARGET) / HY_FAST_DECAY
    min_decay = math.log(HY_TARGET) / HY_SLOW_DECAY
    deltas = np.abs(np.linspace(min_decay, max_decay, HY_DIM))
    decay = np.exp(-t01 * deltas[None])
    p = _perm_rows()
    return zp[p].astype(np.float32), decay[p].astype(np.float32)


@functools.lru_cache(maxsize=None)
def _rope_tables():
    t = np.arange(SEQ)
    row = (t // GRID_W).astype(np.float64)
    col = (t % GRID_W).astype(np.float64)
    inv = ROPE_THETA ** (-np.arange(0, AXIS_ROPE_DIM, 2, dtype=np.float64) / AXIS_ROPE_DIM)
    ar = row[:, None] * inv
    ac = col[:, None] * inv
    ang = np.concatenate([ar, ar, ac, ac], axis=-1)
    ang2 = np.concatenate([ang, ang], axis=-1)
    return np.cos(ang2).astype(np.float32), np.sin(ang2).astype(np.float32)


def _table_bf16(t):
    return jnp.asarray(t, F32).astype(BF16)


def _group_sum_matrix(width):
    g = np.arange(width) // HEAD_DIM
    return (g[:, None] == g[None, :]).astype(np.float32)


def _mod_kernel(c_ref, w_ref, b_ref, o_ref):
    c = c_ref[...]
    sc = c * jax.nn.sigmoid(c)
    o_ref[...] = jnp.dot(sc, w_ref[...], precision=lax.Precision.HIGHEST,
                         preferred_element_type=F32) + b_ref[...]


def _modulation(c16, ada_w, ada_b):
    n = ada_w.shape[1]
    tn = 1536
    return pl.pallas_call(
        _mod_kernel,
        out_shape=jax.ShapeDtypeStruct((16, n), F32),
        grid=(n // tn,),
        in_specs=[pl.BlockSpec((16, D_MODEL), lambda j: (0, 0)),
                  pl.BlockSpec((D_MODEL, tn), lambda j: (0, j)),
                  pl.BlockSpec((1, tn), lambda j: (0, j))],
        out_specs=pl.BlockSpec((16, tn), lambda j: (0, j)),
        compiler_params=_cp(("arbitrary",)),
        name="mod",
    )(c16, ada_w, ada_b.reshape(1, n))


def _norm_mod(x, g, shift, scale):
    ms = jnp.mean(x * x, axis=-1, keepdims=True)
    y = x * lax.rsqrt(ms + EPS) * g
    return y * (1.0 + scale) + shift


def _head_norm(z, gsum_ref, gain):
    zz = z * z
    hi = zz.astype(BF16)
    lo = (zz - hi.astype(F32)).astype(BF16)
    gs = gsum_ref[...]
    ssum = (jnp.dot(hi, gs, preferred_element_type=F32) + jnp.dot(lo, gs, preferred_element_type=F32))
    return z * lax.rsqrt(ssum * (1.0 / HEAD_DIM) + EPS) * gain


def _rope(z, cos, sin):
    outs = []
    lane = lax.broadcasted_iota(jnp.int32, (z.shape[0], LANES), 1)
    first = (lane % AXIS_ROPE_DIM) < (AXIS_ROPE_DIM // 2)
    for t in range(z.shape[1] // LANES):
        zt = z[:, t * LANES:(t + 1) * LANES]
        up = pltpu.roll(zt, LANES - AXIS_ROPE_DIM // 2, axis=1)
        dn = pltpu.roll(zt, AXIS_ROPE_DIM // 2, axis=1)
        rot = jnp.where(first, -up, dn)
        outs.append(zt * cos + rot * sin)
    return outs[0] if len(outs) == 1 else jnp.concatenate(outs, axis=1)


def _pad_pairs(z, ones_lane=False):
    lane = lax.broadcasted_iota(jnp.int32, z.shape, 1)
    lo = jnp.where(lane < HEAD_DIM, z, 0.0)
    hi = jnp.where(lane >= HEAD_DIM, z, 0.0)
    parts = [lo, pltpu.roll(lo, HEAD_DIM, axis=1), pltpu.roll(hi, HEAD_DIM, axis=1), hi]
    if ones_lane:
        parts = [jnp.where(lane == (HEAD_DIM if i % 2 == 0 else 0), 1.0, p) for i, p in enumerate(parts)]
    return parts


def _inproj_kernel(x_ref, sh_ref, sc_ref, g_ref, wq_ref, wk_ref, wv_ref, wh_ref, wg_ref,
                   gq_ref, gk_ref, sq_ref, sk_ref, cos_ref, sin_ref,
                   q_ref, k_ref, v_ref, zh_ref, gate_ref):
    h = _norm_mod(x_ref[0], g_ref[...], sh_ref[0], sc_ref[0]).astype(BF16)
    cos = cos_ref[...]
    sin = sin_ref[...]
    q = jnp.dot(h, wq_ref[...], preferred_element_type=F32)
    q = _rope(_head_norm(q, sq_ref, gq_ref[...]), cos, sin)
    q_ref[0] = (q * Q_SCALE).astype(BF16)
    k = jnp.dot(h, wk_ref[...], preferred_element_type=F32)
    k = _rope(_head_norm(k, sk_ref, gk_ref[...]), cos, sin)
    for i, part in enumerate(_pad_pairs(k)):
        k_ref[0, i] = part.astype(BF16)
    v = jnp.dot(h, wv_ref[...], preferred_element_type=F32)
    for i, part in enumerate(_pad_pairs(v, ones_lane=True)):
        v_ref[0, i] = part.astype(BF16)
    zh_ref[0] = jnp.dot(h, wh_ref[...], preferred_element_type=F32)
    gate_ref[0] = jnp.dot(h, wg_ref[...], preferred_element_type=F32).astype(BF16)


def _inproj(x, sh, sc, g, wq, wk, wv, wh, wg, gq, gk, cos, sin, tm=512):
    B, L, D = x.shape
    full = lambda a: pl.BlockSpec(a.shape, lambda b, i: (0,) * a.ndim)
    row = pl.BlockSpec((1, 1, D), lambda b, i: (b, 0, 0))
    sq = jnp.asarray(_group_sum_matrix(Q_DIM), BF16)
    sk = jnp.asarray(_group_sum_matrix(KV_DIM), BF16)
    return pl.pallas_call(
        _inproj_kernel,
        out_shape=(jax.ShapeDtypeStruct((B, L, Q_DIM), BF16),
                   jax.ShapeDtypeStruct((B, 4, L, LANES), BF16),
                   jax.ShapeDtypeStruct((B, 4, L, LANES), BF16),
                   jax.ShapeDtypeStruct((B, L, HY_COLS), F32),
                   jax.ShapeDtypeStruct((B, L, GATE_COLS), BF16)),
        grid=(B, L // tm),
        in_specs=[pl.BlockSpec((1, tm, D), lambda b, i: (b, i, 0)), row, row, full(g),
                  full(wq), full(wk), full(wv), full(wh), full(wg), full(gq), full(gk), full(sq), full(sk),
                  pl.BlockSpec((tm, LANES), lambda b, i: (i, 0)),
                  pl.BlockSpec((tm, LANES), lambda b, i: (i, 0))],
        out_specs=(pl.BlockSpec((1, tm, Q_DIM), lambda b, i: (b, i, 0)),
                   pl.BlockSpec((1, 4, tm, LANES), lambda b, i: (b, 0, i, 0)),
                   pl.BlockSpec((1, 4, tm, LANES), lambda b, i: (b, 0, i, 0)),
                   pl.BlockSpec((1, tm, HY_COLS), lambda b, i: (b, i, 0)),
                   pl.BlockSpec((1, tm, GATE_COLS), lambda b, i: (b, i, 0))),
        compiler_params=_cp(("arbitrary", "arbitrary")),
        name="inproj",
    )(x, sh, sc, g, wq, wk, wv, wh, wg, gq, gk, sq, sk, cos, sin)


def _ctxkv_kernel(x_ref, sh_ref, sc_ref, g_ref, wk_ref, wv_ref, gk_ref, sk_ref, k_ref, v_ref):
    h = _norm_mod(x_ref[0], g_ref[...], sh_ref[...], sc_ref[...]).astype(BF16)
    k = _head_norm(jnp.dot(h, wk_ref[...], preferred_element_type=F32), sk_ref, gk_ref[...])
    for i, part in enumerate(_pad_pairs(k)):
        k_ref[0, i] = part.astype(BF16)
    v = jnp.dot(h, wv_ref[...], preferred_element_type=F32)
    for i, part in enumerate(_pad_pairs(v, ones_lane=True)):
        v_ref[0, i] = part.astype(BF16)


def _ctxkv(ctx, sh, sc, g, wk, wv, gk):
    B, T, D = ctx.shape
    full = lambda a: pl.BlockSpec(a.shape, lambda b: (0,) * a.ndim)
    sk = jnp.asarray(_group_sum_matrix(KV_DIM), BF16)
    return pl.pallas_call(
        _ctxkv_kernel,
        out_shape=(jax.ShapeDtypeStruct((B, 4, T, LANES), BF16),
                   jax.ShapeDtypeStruct((B, 4, T, LANES), BF16)),
        grid=(B,),
        in_specs=[pl.BlockSpec((1, T, D), lambda b: (b, 0, 0)), full(sh), full(sc), full(g),
                  full(wk), full(wv), full(gk), full(sk)],
        out_specs=(pl.BlockSpec((1, 4, T, LANES), lambda b: (b, 0, 0, 0)),
                   pl.BlockSpec((1, 4, T, LANES), lambda b: (b, 0, 0, 0))),
        compiler_params=_cp(("arbitrary",)),
        name="ctxkv",
    )(ctx, sh, sc, g, wk, wv, gk, sk)


def _attn_kernel(q_ref, ks_ref, kc_ref, vs_ref, vc_ref, o_ref):
    nt = (((1,), (1,)), ((), ()))
    lane = lax.broadcasted_iota(jnp.int32, (q_ref.shape[1], LANES), 1)
    chunks = [(kc_ref, vc_ref, 0, kc_ref.shape[2])]
    chunks += [(ks_ref, vs_ref, c0, ATTN_TK) for c0 in range(0, ks_ref.shape[2], ATTN_TK)]
    for pair in range(2):
        q2 = q_ref[0, :, pair * LANES:(pair + 1) * LANES]
        acc = None
        for hh in range(2):
            m = o = None
            for k_ref, v_ref, c0, n in chunks:
                s = lax.dot_general(q2, k_ref[0, hh, c0:c0 + n, :], nt, preferred_element_type=F32)
                mj = jnp.max(s, axis=-1, keepdims=True)
                m_new = mj if m is None else jnp.maximum(m, mj)
                pv = jnp.dot(jnp.exp2(s - m_new).astype(BF16), v_ref[0, hh, c0:c0 + n, :],
                             preferred_element_type=F32)
                o = pv if o is None else o * jnp.exp2(m - m_new) + pv
                m = m_new
            ones_lane = HEAD_DIM if hh == 0 else 0
            o = jnp.where((lane < HEAD_DIM) == (hh == 0), o / o[:, ones_lane:ones_lane + 1], 0.0)
            acc = o if acc is None else acc + o
        o_ref[0, :, pair * LANES:(pair + 1) * LANES] = acc.astype(BF16)


def _attention(q, ks, kc, vs, vc, tq=512):
    B, L, _ = q.shape
    T = kc.shape[2]
    kv_self = pl.BlockSpec((1, 2, L, LANES), lambda b, h, i: (b, h, 0, 0))
    kv_ctx = pl.BlockSpec((1, 2, T, LANES), lambda b, h, i: (b, h, 0, 0))
    qo = pl.BlockSpec((1, tq, 2 * LANES), lambda b, h, i: (b, i, h))
    return pl.pallas_call(
        _attn_kernel,
        out_shape=jax.ShapeDtypeStruct((B, L, Q_DIM), BF16),
        grid=(B, N_KV_HEADS, L // tq),
        in_specs=[qo, kv_self, kv_ctx, kv_self, kv_ctx],
        out_specs=qo,
        compiler_params=_cp(("arbitrary", "arbitrary", "arbitrary")),
        name="attn",
    )(q, ks, kc, vs, vc)


def _strided(n2):
    return pl.ds(n2, FFT_N2, stride=S_PITCH)


def _filt_kernel(z_ref, dec_ref, w1_ref, b1_ref, w2_ref, b2_ref, w3_ref, b3_ref, w4_ref, fr_ref,
                 ff_ref, f2_ref, kf_ref, a_ref, hf_ref, hb_ref, s_ref):
    hp = lax.Precision.HIGHEST

    @pl.when((pl.program_id(0) == 0) & (pl.program_id(1) == 0))
    def _():
        fr = fr_ref[...]
        a = jnp.sin(fr * (jnp.dot(z_ref[...], w1_ref[...], precision=hp, preferred_element_type=F32) + b1_ref[...]))
        a = jnp.sin(fr * (jnp.dot(a, w2_ref[...], precision=hp, preferred_element_type=F32) + b2_ref[...]))
        a_ref[...] = jnp.sin(
            fr * (jnp.dot(a, w3_ref[...], precision=hp, preferred_element_type=F32) + b3_ref[...]))

    h = jnp.dot(a_ref[...], w4_ref[0], precision=hp, preferred_element_type=F32)
    dec = dec_ref[...]
    fwd = h[:, :LANES] * dec
    bwd = h[:, LANES:] * dec
    norm = (jnp.sum(jnp.abs(fwd), axis=0, keepdims=True) + jnp.sum(jnp.abs(bwd), axis=0, keepdims=True)
            - jnp.abs(bwd[0:1, :]) + EPS)
    inv = 1.0 / norm
    hf_ref[...] = fwd * inv
    hb_ref[...] = bwd * inv

    def outer(n2, carry):
        nb = jnp.where(n2 == 0, 0, FFT_N2 - n2)
        rf = hf_ref[pl.ds(pl.multiple_of(n2 * FFT_NH, FFT_NH), FFT_NH), :]
        rb = hb_ref[pl.ds(pl.multiple_of(nb * FFT_NH, FFT_NH), FFT_NH), :]
        rhs = jnp.concatenate([rf, rb], axis=0).astype(BF16)
        s_ref[_strided(n2), :] = jnp.dot(ff_ref[n2], rhs, preferred_element_type=F32)
        return carry

    lax.fori_loop(0, FFT_N2, outer, 0, unroll=DFT_UNROLL)

    def inner(k1, carry):
        re = s_ref[pl.ds(pl.multiple_of(k1 * S_PITCH, 8), FFT_N2), :]
        im = s_ref[pl.ds(pl.multiple_of((FFT_N1 + k1) * S_PITCH, 8), FFT_N2), :]
        rhs = jnp.concatenate([re, im], axis=0).astype(BF16)
        kf_ref[0, k1] = jnp.dot(f2_ref[...], rhs, preferred_element_type=F32).astype(BF16)
        return carry

    lax.fori_loop(0, FFT_N1, inner, 0, unroll=DFT_UNROLL)


def _filters(hf_w1, hf_b1, hf_w2, hf_b2, hf_w3, hf_b3, hf_w4, hf_freq):
    zp, decay = _filter_consts()
    _, _, f2b, _, ff = _dft_tables()
    w1 = jnp.zeros((LANES, HY_FFN), F32).at[:hf_w1.shape[0]].set(hf_w1)
    nct = HY_DIM // LANES
    w4 = hf_w4.reshape(HY_FFN, HY_ORDER, 2, nct, LANES).transpose(1, 3, 0, 2, 4)
    w4 = w4.reshape(HY_ORDER * nct, HY_FFN, 2 * LANES)
    full = lambda a: pl.BlockSpec(a.shape, lambda o, c: (0,) * a.ndim)
    vec = lambda v: v.reshape(1, HY_FFN)
    args = (jnp.asarray(zp), jnp.asarray(decay), w1, vec(hf_b1), hf_w2, vec(hf_b2), hf_w3, vec(hf_b3), w4,
            vec(hf_freq), _table_bf16(ff), _table_bf16(f2b))
    in_specs = [full(args[0]), pl.BlockSpec((SEQ, LANES), lambda o, c: (0, c))]
    in_specs += [full(a) for a in args[2:8]]
    in_specs += [pl.BlockSpec((1, HY_FFN, 2 * LANES), lambda o, c: (o * nct + c, 0, 0))]
    in_specs += [full(a) for a in args[9:]]
    return pl.pallas_call(
        _filt_kernel,
        out_shape=jax.ShapeDtypeStruct((HY_ORDER, FFT_N1, 2 * FFT_N2, HY_DIM), BF16),
        grid=(HY_ORDER, nct),
        in_specs=in_specs,
        out_specs=pl.BlockSpec((1, FFT_N1, 2 * FFT_N2, LANES), lambda o, c: (o, 0, 0, c)),
        scratch_shapes=[pltpu.VMEM((SEQ, HY_FFN), F32), pltpu.VMEM((SEQ, LANES), F32),
                        pltpu.VMEM((SEQ, LANES), F32), pltpu.VMEM((2 * FFT_N1 * S_PITCH, LANES), F32)],
        compiler_params=_cp(("arbitrary", "arbitrary")),
        name="filt",
    )(*args)


def _conv_kernel(sc_u, u_ref, g_ref, wu_ref, bu_ref, wg_ref, bg_ref, skip_ref, kf_ref,
                 fa_ref, ga_ref, f2_ref, f2i_ref, o_ref, s_ref):
    row = lax.broadcasted_iota(jnp.int32, (FFT_NH, LANES), 0)

    def blk(ref, n2):
        return [ref[b, pl.ds(n2, FFT_NH, stride=FFT_N2), :] for b in range(2)]

    def down(x):
        return jnp.where(row == 0, 0.0, pltpu.roll(x, 1, axis=0))

    def up(x):
        return jnp.where(row == FFT_NH - 1, 0.0, pltpu.roll(x, FFT_NH - 1, axis=0))

    def cat(pair):
        return jnp.concatenate(pair, axis=0)

    def sconv(w_ref, b_ref, prev, cur, nxt):
        return prev * w_ref[0] + cur * w_ref[1] + nxt * w_ref[2] + b_ref[...]

    def first(ref):
        return cat([down(x) for x in blk(ref, FFT_N2 - 1)]), cat(blk(ref, 0))

    def last(ref):
        return cat([up(x) for x in blk(ref, 0)])

    def fwd_store(n2, ublk):
        s_ref[_strided(n2), :] = jnp.dot(fa_ref[n2], ublk.astype(BF16), preferred_element_type=F32)

    if sc_u:
        def fwd_body(n2, carry):
            prev, cur = carry
            nxt = cat(blk(u_ref, n2 + 1))
            fwd_store(n2, sconv(wu_ref, bu_ref, prev, cur, nxt))
            return cur, nxt

        prev, cur = lax.fori_loop(0, FFT_N2 - 1, fwd_body, first(u_ref), unroll=DFT_UNROLL)
        fwd_store(FFT_N2 - 1, sconv(wu_ref, bu_ref, prev, cur, last(u_ref)))
    else:
        def fwd_body(n2, carry):
            fwd_store(n2, cat(blk(u_ref, n2)))
            return carry

        lax.fori_loop(0, FFT_N2, fwd_body, 0, unroll=DFT_UNROLL)

    def mid_body(k1, carry):
        r0 = pl.multiple_of(k1 * S_PITCH, 8)
        r1 = pl.multiple_of((FFT_N1 + k1) * S_PITCH, 8)
        rhs = jnp.concatenate([s_ref[pl.ds(r0, FFT_N2), :], s_ref[pl.ds(r1, FFT_N2), :]], axis=0).astype(BF16)
        x = jnp.dot(f2_ref[...], rhs, preferred_element_type=F32)
        kf = kf_ref[0, k1].astype(F32)
        xr, xi = x[:FFT_N2], x[FFT_N2:]
        kr, ki = kf[:FFT_N2], kf[FFT_N2:]
        y = jnp.concatenate([xr * kr - xi * ki, xr * ki + xi * kr], axis=0).astype(BF16)
        p = jnp.dot(f2i_ref[...], y, preferred_element_type=F32)
        s_ref[pl.ds(r0, FFT_N2), :] = p[:FFT_N2]
        s_ref[pl.ds(r1, FFT_N2), :] = p[FFT_N2:]
        return carry

    lax.fori_loop(0, FFT_N1, mid_body, 0, unroll=DFT_UNROLL)

    skip = skip_ref[0]

    def out_store(n2, ublk, gblk):
        c = jnp.dot(ga_ref[n2], s_ref[_strided(n2), :].astype(BF16), preferred_element_type=F32)
        y = gblk * (c + skip * ublk)
        o_ref[0, pl.ds(n2, FFT_NH, stride=FFT_N2), :] = y[:FFT_NH]
        o_ref[1, pl.ds(n2, FFT_NH, stride=FFT_N2), :] = y[FFT_NH:]

    if sc_u:
        def out_body(n2, carry):
            pu, cu, pg, cg = carry
            nu = cat(blk(u_ref, n2 + 1))
            ng = cat(blk(g_ref, n2 + 1))
            out_store(n2, sconv(wu_ref, bu_ref, pu, cu, nu), sconv(wg_ref, bg_ref, pg, cg, ng))
            return cu, nu, cg, ng

        pu, cu, pg, cg = lax.fori_loop(0, FFT_N2 - 1, out_body, first(u_ref) + first(g_ref),
                                       unroll=DFT_UNROLL)
        out_store(FFT_N2 - 1, sconv(wu_ref, bu_ref, pu, cu, last(u_ref)),
                  sconv(wg_ref, bg_ref, pg, cg, last(g_ref)))
    else:
        def out_body(n2, carry):
            pg, cg = carry
            ng = cat(blk(g_ref, n2 + 1))
            out_store(n2, cat(blk(u_ref, n2)), sconv(wg_ref, bg_ref, pg, cg, ng))
            return cg, ng

        pg, cg = lax.fori_loop(0, FFT_N2 - 1, out_body, first(g_ref), unroll=DFT_UNROLL)
        out_store(FFT_N2 - 1, cat(blk(u_ref, FFT_N2 - 1)), sconv(wg_ref, bg_ref, pg, cg, last(g_ref)))


def _longconv(u, u_part, gate, g_part, conv_w, conv_b, skip, kf, order, sc_u):
    B = u.shape[0]
    nct = HY_DIM // LANES
    fa, ga, f2b, f2ib, _ = _dft_tables()
    fa, ga, f2b, f2ib = (_table_bf16(t) for t in (fa, ga, f2b, f2ib))
    w3 = conv_w.reshape(3, 1, HY_COLS)
    b3 = conv_b.reshape(1, HY_COLS)
    skip2 = skip.reshape(HY_ORDER, 1, HY_DIM)
    full = lambda a: pl.BlockSpec(a.shape, lambda c, p: (0,) * a.ndim)
    sig = lambda part: pl.BlockSpec((2, SEQ, LANES), lambda c, p: (p, 0, part * nct + c))
    wspec = lambda part: pl.BlockSpec((3, 1, LANES), lambda c, p: (0, 0, part * nct + c))
    bspec = lambda part: pl.BlockSpec((1, LANES), lambda c, p: (0, part * nct + c))
    wu_part = u_part if sc_u else g_part
    return pl.pallas_call(
        functools.partial(_conv_kernel, sc_u),
        out_shape=jax.ShapeDtypeStruct((B, SEQ, HY_DIM), F32),
        grid=(nct, B // 2),
        in_specs=[sig(u_part), sig(g_part), wspec(wu_part), bspec(wu_part), wspec(g_part), bspec(g_part),
                  pl.BlockSpec((1, 1, LANES), lambda c, p: (order, 0, c)),
                  pl.BlockSpec((1, FFT_N1, 2 * FFT_N2, LANES), lambda c, p: (order, 0, 0, c)),
                  full(fa), full(ga), full(f2b), full(f2ib)],
        out_specs=pl.BlockSpec((2, SEQ, LANES), lambda c, p: (p, 0, c)),
        scratch_shapes=[pltpu.VMEM((2 * FFT_N1 * S_PITCH, LANES), F32)],
        compiler_params=_cp(("arbitrary", "arbitrary")),
        name="conv%d" % order,
    )(u, gate, w3, b3, w3, b3, skip2, kf, fa, ga, f2b, f2ib)


def _merge_kernel(x_ref, at_ref, hy_ref, gt_ref, g1_ref, sh_ref, sc_ref, ng_ref, wa_ref, wh_ref, wo_ref,
                  x1_ref, h2_ref):
    ga = jax.nn.sigmoid(gt_ref[0, :, :D_MODEL].astype(F32))
    gh = jax.nn.sigmoid(gt_ref[0, :, D_MODEL:].astype(F32))
    merged = (ga * jnp.dot(at_ref[0], wa_ref[...], preferred_element_type=F32)
              + gh * jnp.dot(hy_ref[0].astype(BF16), wh_ref[...], preferred_element_type=F32))
    out = jnp.dot(merged.astype(BF16), wo_ref[...], preferred_element_type=F32)
    x1 = x_ref[0] + g1_ref[0] * out
    x1_ref[0] = x1
    h2_ref[0] = _norm_mod(x1, ng_ref[...], sh_ref[0], sc_ref[0]).astype(BF16)


def _merge(x, attn, hy, gates, g1, sh2, sc2, ng, wa, wh, wo, tm=512):
    B, L, D = x.shape
    full = lambda a: pl.BlockSpec(a.shape, lambda b, i: (0,) * a.ndim)
    row = pl.BlockSpec((1, 1, D), lambda b, i: (b, 0, 0))
    tok = lambda w: pl.BlockSpec((1, tm, w), lambda b, i: (b, i, 0))
    return pl.pallas_call(
        _merge_kernel,
        out_shape=(jax.ShapeDtypeStruct((B, L, D), F32), jax.ShapeDtypeStruct((B, L, D), BF16)),
        grid=(B, L // tm),
        in_specs=[tok(D), tok(Q_DIM), tok(HY_DIM), tok(GATE_COLS), row, row, row, full(ng),
                  full(wa), full(wh), full(wo)],
        out_specs=(tok(D), tok(D)),
        compiler_params=_cp(("arbitrary", "arbitrary")),
        name="merge",
    )(x, attn, hy, gates, g1, sh2, sc2, ng, wa, wh, wo)


def _top16_rows(s, pos, sentinel, idx=None):
    vals, pays = [], []
    for _ in range(PEER_TOPK):
        m = jnp.max(s, axis=0, keepdims=True)
        first = jnp.min(jnp.where(s == m, pos, sentinel), axis=0, keepdims=True)
        hit = pos == first
        pays.append(first if idx is None else jnp.sum(jnp.where(hit, idx, 0.0), axis=0, keepdims=True))
        vals.append(m)
        s = jnp.where(hit, -jnp.inf, s)
    return jnp.concatenate(vals, axis=0), jnp.concatenate(pays, axis=0)


_CAND_GROUPS = [(0, 0, 8), (0, 8, 8), (1, 0, 8), (2, 0, 5), (3, 0, 4), (4, 0, 3), (5, 0, 2), (6, 0, 2), (7, 0, 2)]


def _pair_candidates(v1, i1, v2, i2):
    tt = v1.shape[1]
    r8 = lax.broadcasted_iota(jnp.int32, (8, tt), 0)
    r8f = r8.astype(F32)
    cand, pos, cidx = [], [], []
    for k1, k2, valid in _CAND_GROUPS:
        c = v1[k1:k1 + 1] + v2[k2:k2 + 8]
        cand.append(c if valid == 8 else jnp.where(r8 < valid, c, -jnp.inf))
        pos.append(r8f + float(k1 * PEER_TOPK + k2))
        cidx.append(i1[k1:k1 + 1] * float(PEER_N_KEYS) + i2[k2:k2 + 8])
    cand.append(v1[8:16] + v2[0:1])
    pos.append((r8f + 8.0) * float(PEER_TOPK))
    cidx.append(i1[8:16] * float(PEER_N_KEYS) + i2[0:1])
    return jnp.concatenate(cand, axis=0), jnp.concatenate(pos, axis=0), jnp.concatenate(cidx, axis=0)


def _topk_kernel(h_ref, wq_ref, k1_ref, k2_ref, e_ref, g_ref):
    tt = h_ref.shape[0]
    nt = (((1,), (1,)), ((), ()))
    xq = lax.dot_general(wq_ref[...], h_ref[...], nt, preferred_element_type=F32)
    key_id = lax.broadcasted_iota(jnp.int32, (PEER_N_KEYS, tt), 0).astype(F32)
    half = PEER_DK // 2
    for h in range(PEER_HEADS):
        q1 = xq[h * PEER_DK:h * PEER_DK + half].astype(BF16)
        q2 = xq[h * PEER_DK + half:(h + 1) * PEER_DK].astype(BF16)
        s1 = jnp.dot(k1_ref[h], q1, preferred_element_type=F32)
        s2 = jnp.dot(k2_ref[h], q2, preferred_element_type=F32)
        v1, i1 = _top16_rows(s1, key_id, float(PEER_N_KEYS))
        v2, i2 = _top16_rows(s2, key_id, float(PEER_N_KEYS))
        cand, pos, cidx = _pair_candidates(v1, i1, v2, i2)
        best, eidx = _top16_rows(cand, pos, float(PEER_TOPK * PEER_TOPK), cidx)
        p = jnp.exp(best - best[0:1])
        g_ref[h * PEER_TOPK:(h + 1) * PEER_TOPK, :] = p / jnp.sum(p, axis=0, keepdims=True)
        e_ref[h * PEER_TOPK:(h + 1) * PEER_TOPK, :] = eidx


def _peer_topk(h2, wq_t, keys1, keys2, tt=256):
    n = h2.shape[0]
    full = lambda a: pl.BlockSpec(a.shape, lambda i: (0,) * a.ndim)
    out = jax.ShapeDtypeStruct((PEER_HEADS * PEER_TOPK, n), F32)
    ospec = pl.BlockSpec((PEER_HEADS * PEER_TOPK, tt), lambda i: (0, i))
    return pl.pallas_call(
        _topk_kernel,
        out_shape=(out, out),
        grid=(n // tt,),
        in_specs=[pl.BlockSpec((tt, D_MODEL), lambda i: (i, 0)), full(wq_t), full(keys1), full(keys2)],
        out_specs=(ospec, ospec),
        compiler_params=_cp(("arbitrary",)),
        name="topk",
    )(h2, wq_t, keys1, keys2)


def _gmat_kernel(e_ref, g_ref, o_ref, s_ref):
    tm = e_ref.shape[0]
    sub = 32
    nk = PEER_N_KEYS
    ids = lax.broadcasted_iota(jnp.int32, (sub, nk, nk), 1).astype(F32)
    for c0 in range(0, tm, sub):
        e = e_ref[c0:c0 + sub, :]
        a = jnp.floor(e * (1.0 / nk))
        b = e - a * nk
        g = g_ref[c0:c0 + sub, :]
        at = jnp.where(ids == a[:, None, :], 1.0, 0.0).astype(BF16)
        bt = jnp.where(ids == b[:, None, :], g[:, None, :], 0.0).astype(BF16)
        g3 = jnp.einsum("caj,cbj->cab", at, bt, preferred_element_type=F32)
        for c in range(sub):
            s_ref[pl.ds(c0 + c, nk, stride=S_PITCH), :] = g3[c]
    for a in range(nk):
        o_ref[a] = s_ref[a * S_PITCH:a * S_PITCH + tm, :].astype(BF16)


def _peer_gmat(e, g, tm=128):
    n = e.shape[0]
    return pl.pallas_call(
        _gmat_kernel,
        out_shape=jax.ShapeDtypeStruct((PEER_N_KEYS, n, PEER_N_KEYS), BF16),
        grid=(n // tm,),
        in_specs=[pl.BlockSpec((tm, LANES), lambda i: (i, 0)), pl.BlockSpec((tm, LANES), lambda i: (i, 0))],
        out_specs=pl.BlockSpec((PEER_N_KEYS, tm, PEER_N_KEYS), lambda i: (0, i, 0)),
        scratch_shapes=[pltpu.VMEM((PEER_N_KEYS * S_PITCH, LANES), F32)],
        compiler_params=_cp(("arbitrary",)),
        name="gmat",
    )(e, g)


def _dense_kernel(h_ref, ut_ref, v_ref, gm_ref, x1_ref, g2_ref, fg_ref, o_ref, acc_ref):
    j = pl.program_id(1)

    @pl.when(j == 0)
    def _():
        acc_ref[...] = jnp.zeros_like(acc_ref)

    a = jnp.dot(h_ref[...], ut_ref[...], preferred_element_type=F32)
    na = gm_ref.shape[0]
    w = jnp.concatenate(
        [jax.nn.gelu(a[:, i * LANES:(i + 1) * LANES], approximate=True) * gm_ref[i].astype(F32)
         for i in range(na)], axis=1).astype(BF16)
    acc_ref[...] += jnp.dot(w, v_ref[...], preferred_element_type=F32)

    @pl.when(j == pl.num_programs(1) - 1)
    def _():
        x2 = x1_ref[...] + g2_ref[0] * acc_ref[...]
        ms = jnp.mean(x2 * x2, axis=-1, keepdims=True)
        o_ref[...] = x2 * lax.rsqrt(ms + EPS) * fg_ref[...]


def _peer_dense(h2, u_t, v, gm, x1, g2, fg, seq, tm=512, ne=2048):
    n, d = h2.shape
    n_exp = v.shape[0]
    per_b = seq // tm
    return pl.pallas_call(
        _dense_kernel,
        out_shape=jax.ShapeDtypeStruct((n, d), F32),
        grid=(n // tm, n_exp // ne),
        in_specs=[pl.BlockSpec((tm, d), lambda i, j: (i, 0)),
                  pl.BlockSpec((d, ne), lambda i, j: (0, j)),
                  pl.BlockSpec((ne, d), lambda i, j: (j, 0)),
                  pl.BlockSpec((ne // LANES, tm, LANES), lambda i, j: (j, i, 0)),
                  pl.BlockSpec((tm, d), lambda i, j: (i, 0)),
                  pl.BlockSpec((1, 1, d), lambda i, j: (i // per_b, 0, 0)),
                  pl.BlockSpec((1, d), lambda i, j: (0, 0))],
        out_specs=pl.BlockSpec((tm, d), lambda i, j: (i, 0)),
        scratch_shapes=[pltpu.VMEM((tm, d), F32)],
        compiler_params=_cp(("arbitrary", "arbitrary")),
        name="dense",
    )(h2, u_t, v, gm, x1, g2, fg)


def kernel(x, c, ctx, c_ctx, ada_w, ada_b, norm_mix_g, norm_ffn_g, w_in, q_norm_g, k_norm_g, hy_conv_w, hy_conv_b, hf_w1, hf_b1, hf_w2, hf_b2, hf_w3, hf_b3, hf_w4, hf_freq, hy_skip, w_attn_out, w_hy_out, w_out, peer_wq, peer_keys1, peer_keys2, peer_u, peer_v, final_norm_g):
    B, L, D = x.shape
    assert L == SEQ and D == D_MODEL and B % 2 == 0 and ada_w.shape[0] == 1
    bf = lambda a: a.astype(BF16)

    c16 = jnp.zeros((16, D), F32).at[:B].set(c).at[B].set(c_ctx)
    mod = _modulation(c16, ada_w[0], ada_b[0])
    part = lambda i: mod[:B, i * D:(i + 1) * D].reshape(B, 1, D)
    sh1, sc1, g1, sh2, sc2, g2 = (part(i) for i in range(6))
    csh1, csc1 = mod[B:B + 1, 0:D], mod[B:B + 1, D:2 * D]

    w = w_in[0]
    o_k, o_v, o_h, o_g = Q_DIM, Q_DIM + KV_DIM, Q_DIM + 2 * KV_DIM, Q_DIM + 2 * KV_DIM + HY_COLS
    wq, wk, wv, wh, wg = bf(w[:, :o_k]), bf(w[:, o_k:o_v]), bf(w[:, o_v:o_h]), bf(w[:, o_h:o_g]), bf(w[:, o_g:])
    gq = jnp.tile(q_norm_g[0], N_HEADS).reshape(1, Q_DIM)
    gk = jnp.tile(k_norm_g[0], N_KV_HEADS).reshape(1, KV_DIM)
    ng1 = norm_mix_g[0].reshape(1, D)
    cos, sin = (jnp.asarray(t) for t in _rope_tables())

    q, ks, vs, zhy, gates = _inproj(x, sh1, sc1, ng1, wq, wk, wv, wh, wg, gq, gk, cos, sin)
    kc, vc = _ctxkv(ctx, csh1, csc1, ng1, wk, wv, gk)
    attn = _attention(q, ks, kc, vs, vc)

    kf = _filters(hf_w1[0], hf_b1[0], hf_w2[0], hf_b2[0], hf_w3[0], hf_b3[0], hf_w4[0], hf_freq[0])
    y1 = _longconv(zhy, 0, zhy, 1, hy_conv_w[0], hy_conv_b[0], hy_skip[0], kf, 0, True)
    y2 = _longconv(y1, 0, zhy, 2, hy_conv_w[0], hy_conv_b[0], hy_skip[0], kf, 1, False)

    x1, h2 = _merge(x, attn, y2, gates, g1, sh2, sc2, norm_ffn_g[0].reshape(1, D),
                    bf(w_attn_out[0]), bf(w_hy_out[0]), bf(w_out[0]))

    n = B * L
    h2f = h2.reshape(n, D)
    e_t, g_t = _peer_topk(h2f, bf(peer_wq[0].T), bf(peer_keys1[0]), bf(peer_keys2[0]))
    gm = _peer_gmat(e_t.T, g_t.T)
    out = _peer_dense(h2f, bf(peer_u[0].T), bf(peer_v[0]), gm, x1.reshape(n, D), g2,
                      final_norm_g.reshape(1, D), L)
    return out.reshape(B, L, D)
```

```python
import functools
import math

import numpy as np
import jax
import jax.numpy as jnp
from jax import lax
from jax.experimental import pallas as pl
from jax.experimental.pallas import tpu as pltpu

F32 = jnp.float32
BF16 = jnp.bfloat16

D_MODEL = 1024
GRID_W = 64
EPS = 1e-6
N_HEADS = 8
N_KV_HEADS = 2
HEAD_DIM = 64
AXIS_ROPE_DIM = HEAD_DIM // 2
ROPE_THETA = 10000.0
HY_DIM = D_MODEL // 2
HY_ORDER = 2
HY_EMB_BANDS = 16
HY_FFN = 64
HY_FAST_DECAY = 0.3
HY_SLOW_DECAY = 1.5
HY_TARGET = 1e-2
Q_DIM = N_HEADS * HEAD_DIM
KV_DIM = N_KV_HEADS * HEAD_DIM
HY_COLS = (HY_ORDER + 1) * HY_DIM
GATE_COLS = 2 * D_MODEL
PEER_HEADS = 8
PEER_N_KEYS = 128
PEER_DK = 256
PEER_TOPK = 16

LANES = 128
SEQ = 4096
FFT_N = 2 * SEQ
FFT_N1 = 64
FFT_N2 = 128
FFT_NH = FFT_N1 // 2
S_PITCH = FFT_N2 + 8
VMEM_LIMIT = 56 * 1024 * 1024
ATTN_TK = 2048
DFT_UNROLL = 8
Q_SCALE = HEAD_DIM ** -0.5 * math.log2(math.e)


def _cp(sem, vmem=None):
    return pltpu.CompilerParams(dimension_semantics=sem, vmem_limit_bytes=vmem or VMEM_LIMIT)


def _blk(c):
    return np.block([[c.real, -c.imag], [c.imag, c.real]])


@functools.lru_cache(maxsize=None)
def _dft_tables():
    n2 = np.arange(FFT_N2)
    k1 = np.arange(FFT_N1)
    n1 = np.arange(FFT_N1)
    w64 = np.exp(-2j * np.pi * np.outer(k1, n1) / FFT_N1)
    tw = np.exp(-2j * np.pi * np.outer(n2, k1) / FFT_N)
    full = w64[None, :, :] * tw[:, :, None]
    fa = np.stack([_blk(full[i, :, :FFT_NH]) for i in range(FFT_N2)])
    ga = np.stack([_blk(np.conj(full[i, :, :FFT_NH]).T / FFT_N) for i in range(FFT_N2)])
    f2 = np.exp(-2j * np.pi * np.outer(np.arange(FFT_N2), np.arange(FFT_N2)) / FFT_N2)
    f2b = _blk(f2)
    f2ib = _blk(np.conj(f2))
    ff = np.zeros((FFT_N2, 2 * FFT_N1, 2 * FFT_NH))
    for i in range(FFT_N2):
        cols = np.zeros((FFT_N1, 2 * FFT_NH), dtype=np.complex128)
        cols[:, :FFT_NH] = full[i, :, :FFT_NH]
        r = np.arange(FFT_NH)
        if i == 0:
            cols[:, FFT_NH + 1:] = full[i][:, FFT_N1 - r[1:]]
        else:
            cols[:, FFT_NH:] = full[i][:, FFT_N1 - 1 - r]
        ff[i] = np.concatenate([cols.real, cols.imag], axis=0)
    return fa, ga, f2b, f2ib, ff


@functools.lru_cache(maxsize=None)
def _perm_rows():
    n2 = np.arange(FFT_N2)[:, None]
    n1 = np.arange(FFT_NH)[None, :]
    return (FFT_N2 * n1 + n2).reshape(-1)


@functools.lru_cache(maxsize=None)
def _filter_consts():
    L = SEQ
    t01 = np.linspace(0.0, 1.0, L)[:, None]
    w = 2.0 * np.pi * np.arange(L)[:, None] / L
    f = np.linspace(1e-4, HY_EMB_BANDS - 1, HY_EMB_BANDS)[None]
    z = np.concatenate([t01, np.cos(f * w), -np.sin(f * w)], axis=-1)
    zp = np.zeros((L, LANES))
    zp[:, :z.shape[1]] = z
    max_decay = math.log(HY_TARGET) / HY_FAST_DECAY
    min_decay = math.log(HY_TARGET) / HY_SLOW_DECAY
    deltas = np.abs(np.linspace(min_decay, max_decay, HY_DIM))
    decay = np.exp(-t01 * deltas[None])
    p = _perm_rows()
    return zp[p].astype(np.float32), decay[p].astype(np.float32)


@functools.lru_cache(maxsize=None)
def _rope_tables():
    t = np.arange(SEQ)
    row = (t // GRID_W).astype(np.float64)
    col = (t % GRID_W).astype(np.float64)
    inv = ROPE_THETA ** (-np.arange(0, AXIS_ROPE_DIM, 2, dtype=np.float64) / AXIS_ROPE_DIM)
    ar = row[:, None] * inv
    ac = col[:, None] * inv
    ang = np.concatenate([ar, ar, ac, ac], axis=-1)
    ang2 = np.concatenate([ang, ang], axis=-1)
    return np.cos(ang2).astype(np.float32), np.sin(ang2).astype(np.float32)


def _table_bf16(t):
    return jnp.asarray(t, F32).astype(BF16)


def _group_sum_matrix(width):
    g = np.arange(width) // HEAD_DIM
    return (g[:, None] == g[None, :]).astype(np.float32)


def _mod_kernel(c_ref, w_ref, b_ref, o_ref):
    c = c_ref[...]
    sc = c * jax.nn.sigmoid(c)
    o_ref[...] = jnp.dot(sc, w_ref[...], precision=lax.Precision.HIGHEST,
                         preferred_element_type=F32) + b_ref[...]


def _modulation(c16, ada_w, ada_b):
    n = ada_w.shape[1]
    tn = 1536
    return pl.pallas_call(
        _mod_kernel,
        out_shape=jax.ShapeDtypeStruct((16, n), F32),
        grid=(n // tn,),
        in_specs=[pl.BlockSpec((16, D_MODEL), lambda j: (0, 0)),
                  pl.BlockSpec((D_MODEL, tn), lambda j: (0, j)),
                  pl.BlockSpec((1, tn), lambda j: (0, j))],
        out_specs=pl.BlockSpec((16, tn), lambda j: (0, j)),
        compiler_params=_cp(("arbitrary",)),
        name="mod",
    )(c16, ada_w, ada_b.reshape(1, n))


def _norm_mod(x, g, shift, scale):
    ms = jnp.mean(x * x, axis=-1, keepdims=True)
    y = x * lax.rsqrt(ms + EPS) * g
    return y * (1.0 + scale) + shift


def _head_norm(z, gsum_ref, gain):
    zz = z * z
    hi = zz.astype(BF16)
    lo = (zz - hi.astype(F32)).astype(BF16)
    gs = gsum_ref[...]
    ssum = (jnp.dot(hi, gs, preferred_element_type=F32) + jnp.dot(lo, gs, preferred_element_type=F32))
    return z * lax.rsqrt(ssum * (1.0 / HEAD_DIM) + EPS) * gain


def _rope(z, cos, sin):
    outs = []
    lane = lax.broadcasted_iota(jnp.int32, (z.shape[0], LANES), 1)
    first = (lane % AXIS_ROPE_DIM) < (AXIS_ROPE_DIM // 2)
    for t in range(z.shape[1] // LANES):
        zt = z[:, t * LANES:(t + 1) * LANES]
        up = pltpu.roll(zt, LANES - AXIS_ROPE_DIM // 2, axis=1)
        dn = pltpu.roll(zt, AXIS_ROPE_DIM // 2, axis=1)
        rot = jnp.where(first, -up, dn)
        outs.append(zt * cos + rot * sin)
    return outs[0] if len(outs) == 1 else jnp.concatenate(outs, axis=1)


def _pad_pairs(z, ones_lane=False):
    lane = lax.broadcasted_iota(jnp.int32, z.shape, 1)
    lo = jnp.where(lane < HEAD_DIM, z, 0.0)
    hi = jnp.where(lane >= HEAD_DIM, z, 0.0)
    parts = [lo, pltpu.roll(lo, HEAD_DIM, axis=1), pltpu.roll(hi, HEAD_DIM, axis=1), hi]
    if ones_lane:
        parts = [jnp.where(lane == (HEAD_DIM if i % 2 == 0 else 0), 1.0, p) for i, p in enumerate(parts)]
    return parts


def _inproj_kernel(x_ref, sh_ref, sc_ref, g_ref, wq_ref, wk_ref, wv_ref, wh_ref, wg_ref,
                   gq_ref, gk_ref, sq_ref, sk_ref, cos_ref, sin_ref,
                   q_ref, k_ref, v_ref, zh_ref, gate_ref):
    h = _norm_mod(x_ref[0], g_ref[...], sh_ref[0], sc_ref[0]).astype(BF16)
    cos = cos_ref[...]
    sin = sin_ref[...]
    q = jnp.dot(h, wq_ref[...], preferred_element_type=F32)
    q = _rope(_head_norm(q, sq_ref, gq_ref[...]), cos, sin)
    q_ref[0] = (q * Q_SCALE).astype(BF16)
    k = jnp.dot(h, wk_ref[...], preferred_element_type=F32)
    k = _rope(_head_norm(k, sk_ref, gk_ref[...]), cos, sin)
    for i, part in enumerate(_pad_pairs(k)):
        k_ref[0, i] = part.astype(BF16)
    v = jnp.dot(h, wv_ref[...], preferred_element_type=F32)
    for i, part in enumerate(_pad_pairs(v, ones_lane=True)):
        v_ref[0, i] = part.astype(BF16)
    zh_ref[0] = jnp.dot(h, wh_ref[...], preferred_element_type=F32)
    gate_ref[0] = jnp.dot(h, wg_ref[...], preferred_element_type=F32).astype(BF16)


def _inproj(x, sh, sc, g, wq, wk, wv, wh, wg, gq, gk, cos, sin, tm=512):
    B, L, D = x.shape
    full = lambda a: pl.BlockSpec(a.shape, lambda b, i: (0,) * a.ndim)
    row = pl.BlockSpec((1, 1, D), lambda b, i: (b, 0, 0))
    sq = jnp.asarray(_group_sum_matrix(Q_DIM), BF16)
    sk = jnp.asarray(_group_sum_matrix(KV_DIM), BF16)
    return pl.pallas_call(
        _inproj_kernel,
        out_shape=(jax.ShapeDtypeStruct((B, L, Q_DIM), BF16),
                   jax.ShapeDtypeStruct((B, 4, L, LANES), BF16),
                   jax.ShapeDtypeStruct((B, 4, L, LANES), BF16),
                   jax.ShapeDtypeStruct((B, L, HY_COLS), F32),
                   jax.ShapeDtypeStruct((B, L, GATE_COLS), BF16)),
        grid=(B, L // tm),
        in_specs=[pl.BlockSpec((1, tm, D), lambda b, i: (b, i, 0)), row, row, full(g),
                  full(wq), full(wk), full(wv), full(wh), full(wg), full(gq), full(gk), full(sq), full(sk),
                  pl.BlockSpec((tm, LANES), lambda b, i: (i, 0)),
                  pl.BlockSpec((tm, LANES), lambda b, i: (i, 0))],
        out_specs=(pl.BlockSpec((1, tm, Q_DIM), lambda b, i: (b, i, 0)),
                   pl.BlockSpec((1, 4, tm, LANES), lambda b, i: (b, 0, i, 0)),
                   pl.BlockSpec((1, 4, tm, LANES), lambda b, i: (b, 0, i, 0)),
                   pl.BlockSpec((1, tm, HY_COLS), lambda b, i: (b, i, 0)),
                   pl.BlockSpec((1, tm, GATE_COLS), lambda b, i: (b, i, 0))),
        compiler_params=_cp(("arbitrary", "arbitrary")),
        name="inproj",
    )(x, sh, sc, g, wq, wk, wv, wh, wg, gq, gk, sq, sk, cos, sin)


def _ctxkv_kernel(x_ref, sh_ref, sc_ref, g_ref, wk_ref, wv_ref, gk_ref, sk_ref, k_ref, v_ref):
    h = _norm_mod(x_ref[0], g_ref[...], sh_ref[...], sc_ref[...]).astype(BF16)
    k = _head_norm(jnp.dot(h, wk_ref[...], preferred_element_type=F32), sk_ref, gk_ref[...])
    for i, part in enumerate(_pad_pairs(k)):
        k_ref[0, i] = part.astype(BF16)
    v = jnp.dot(h, wv_ref[...], preferred_element_type=F32)
    for i, part in enumerate(_pad_pairs(v, ones_lane=True)):
        v_ref[0, i] = part.astype(BF16)


def _ctxkv(ctx, sh, sc, g, wk, wv, gk):
    B, T, D = ctx.shape
    full = lambda a: pl.BlockSpec(a.shape, lambda b: (0,) * a.ndim)
    sk = jnp.asarray(_group_sum_matrix(KV_DIM), BF16)
    return pl.pallas_call(
        _ctxkv_kernel,
        out_shape=(jax.ShapeDtypeStruct((B, 4, T, LANES), BF16),
                   jax.ShapeDtypeStruct((B, 4, T, LANES), BF16)),
        grid=(B,),
        in_specs=[pl.BlockSpec((1, T, D), lambda b: (b, 0, 0)), full(sh), full(sc), full(g),
                  full(wk), full(wv), full(gk), full(sk)],
        out_specs=(pl.BlockSpec((1, 4, T, LANES), lambda b: (b, 0, 0, 0)),
                   pl.BlockSpec((1, 4, T, LANES), lambda b: (b, 0, 0, 0))),
        compiler_params=_cp(("arbitrary",)),
        name="ctxkv",
    )(ctx, sh, sc, g, wk, wv, gk, sk)


def _attn_kernel(q_ref, ks_ref, kc_ref, vs_ref, vc_ref, o_ref):
    nt = (((1,), (1,)), ((), ()))
    lane = lax.broadcasted_iota(jnp.int32, (q_ref.shape[1], LANES), 1)
    chunks = [[(ks_ref, vs_ref, c0, ATTN_TK)] for c0 in range(0, ks_ref.shape[2], ATTN_TK)]
    chunks[0].insert(0, (kc_ref, vc_ref, 0, kc_ref.shape[2]))
    for pair in range(2):
        q2 = q_ref[0, :, pair * LANES:(pair + 1) * LANES]
        acc = None
        for hh in range(2):
            m = o = None
            for segs in chunks:
                ss = [lax.dot_general(q2, k_ref[0, hh, c0:c0 + n, :], nt, preferred_element_type=F32)
                      for k_ref, _, c0, n in segs]
                mj = functools.reduce(jnp.maximum, [jnp.max(s, axis=-1, keepdims=True) for s in ss])
                m_new = mj if m is None else jnp.maximum(m, mj)
                pv = sum(jnp.dot(jnp.exp2(s - m_new).astype(BF16), v_ref[0, hh, c0:c0 + n, :],
                                 preferred_element_type=F32) for s, (_, v_ref, c0, n) in zip(ss, segs))
                o = pv if o is None else o * jnp.exp2(m - m_new) + pv
                m = m_new
            ones_lane = HEAD_DIM if hh == 0 else 0
            o = jnp.where((lane < HEAD_DIM) == (hh == 0), o / o[:, ones_lane:ones_lane + 1], 0.0)
            acc = o if acc is None else acc + o
        o_ref[0, :, pair * LANES:(pair + 1) * LANES] = acc.astype(BF16)


def _attention(q, ks, kc, vs, vc, tq=512):
    B, L, _ = q.shape
    T = kc.shape[2]
    kv_self = pl.BlockSpec((1, 2, L, LANES), lambda b, h, i: (b, h, 0, 0))
    kv_ctx = pl.BlockSpec((1, 2, T, LANES), lambda b, h, i: (b, h, 0, 0))
    qo = pl.BlockSpec((1, tq, 2 * LANES), lambda b, h, i: (b, i, h))
    return pl.pallas_call(
        _attn_kernel,
        out_shape=jax.ShapeDtypeStruct((B, L, Q_DIM), BF16),
        grid=(B, N_KV_HEADS, L // tq),
        in_specs=[qo, kv_self, kv_ctx, kv_self, kv_ctx],
        out_specs=qo,
        compiler_params=_cp(("arbitrary", "arbitrary", "arbitrary")),
        name="attn",
    )(q, ks, kc, vs, vc)


def _strided(n2):
    return pl.ds(n2, FFT_N2, stride=S_PITCH)


def _filt_kernel(z_ref, dec_ref, w1_ref, b1_ref, w2_ref, b2_ref, w3_ref, b3_ref, w4_ref, fr_ref,
                 ff_ref, f2_ref, kf_ref, a_ref, hf_ref, hb_ref, s_ref):
    hp = lax.Precision.HIGHEST

    @pl.when((pl.program_id(0) == 0) & (pl.program_id(1) == 0))
    def _():
        fr = fr_ref[...]
        a = jnp.sin(fr * (jnp.dot(z_ref[...], w1_ref[...], precision=hp, preferred_element_type=F32) + b1_ref[...]))
        a = jnp.sin(fr * (jnp.dot(a, w2_ref[...], precision=hp, preferred_element_type=F32) + b2_ref[...]))
        a_ref[...] = jnp.sin(
            fr * (jnp.dot(a, w3_ref[...], precision=hp, preferred_element_type=F32) + b3_ref[...]))

    h = jnp.dot(a_ref[...], w4_ref[0], precision=hp, preferred_element_type=F32)
    dec = dec_ref[...]
    fwd = h[:, :LANES] * dec
    bwd = h[:, LANES:] * dec
    norm = (jnp.sum(jnp.abs(fwd), axis=0, keepdims=True) + jnp.sum(jnp.abs(bwd), axis=0, keepdims=True)
            - jnp.abs(bwd[0:1, :]) + EPS)
    inv = 1.0 / norm
    hf_ref[...] = fwd * inv
    hb_ref[...] = bwd * inv

    def outer(n2, carry):
        nb = jnp.where(n2 == 0, 0, FFT_N2 - n2)
        rf = hf_ref[pl.ds(pl.multiple_of(n2 * FFT_NH, FFT_NH), FFT_NH), :]
        rb = hb_ref[pl.ds(pl.multiple_of(nb * FFT_NH, FFT_NH), FFT_NH), :]
        rhs = jnp.concatenate([rf, rb], axis=0).astype(BF16)
        s_ref[_strided(n2), :] = jnp.dot(ff_ref[n2], rhs, preferred_element_type=F32)
        return carry

    lax.fori_loop(0, FFT_N2, outer, 0, unroll=DFT_UNROLL)

    def inner(k1, carry):
        re = s_ref[pl.ds(pl.multiple_of(k1 * S_PITCH, 8), FFT_N2), :]
        im = s_ref[pl.ds(pl.multiple_of((FFT_N1 + k1) * S_PITCH, 8), FFT_N2), :]
        rhs = jnp.concatenate([re, im], axis=0).astype(BF16)
        kf_ref[0, k1] = jnp.dot(f2_ref[...], rhs, preferred_element_type=F32).astype(BF16)
        return carry

    lax.fori_loop(0, FFT_N1, inner, 0, unroll=DFT_UNROLL)


def _filters(hf_w1, hf_b1, hf_w2, hf_b2, hf_w3, hf_b3, hf_w4, hf_freq):
    zp, decay = _filter_consts()
    _, _, f2b, _, ff = _dft_tables()
    w1 = jnp.zeros((LANES, HY_FFN), F32).at[:hf_w1.shape[0]].set(hf_w1)
    nct = HY_DIM // LANES
    w4 = hf_w4.reshape(HY_FFN, HY_ORDER, 2, nct, LANES).transpose(1, 3, 0, 2, 4)
    w4 = w4.reshape(HY_ORDER * nct, HY_FFN, 2 * LANES)
    full = lambda a: pl.BlockSpec(a.shape, lambda o, c: (0,) * a.ndim)
    vec = lambda v: v.reshape(1, HY_FFN)
    args = (jnp.asarray(zp), jnp.asarray(decay), w1, vec(hf_b1), hf_w2, vec(hf_b2), hf_w3, vec(hf_b3), w4,
            vec(hf_freq), _table_bf16(ff), _table_bf16(f2b))
    in_specs = [full(args[0]), pl.BlockSpec((SEQ, LANES), lambda o, c: (0, c))]
    in_specs += [full(a) for a in args[2:8]]
    in_specs += [pl.BlockSpec((1, HY_FFN, 2 * LANES), lambda o, c: (o * nct + c, 0, 0))]
    in_specs += [full(a) for a in args[9:]]
    return pl.pallas_call(
        _filt_kernel,
        out_shape=jax.ShapeDtypeStruct((HY_ORDER, FFT_N1, 2 * FFT_N2, HY_DIM), BF16),
        grid=(HY_ORDER, nct),
        in_specs=in_specs,
        out_specs=pl.BlockSpec((1, FFT_N1, 2 * FFT_N2, LANES), lambda o, c: (o, 0, 0, c)),
        scratch_shapes=[pltpu.VMEM((SEQ, HY_FFN), F32), pltpu.VMEM((SEQ, LANES), F32),
                        pltpu.VMEM((SEQ, LANES), F32), pltpu.VMEM((2 * FFT_N1 * S_PITCH, LANES), F32)],
        compiler_params=_cp(("arbitrary", "arbitrary")),
        name="filt",
    )(*args)


def _conv_kernel(sc_u, u_ref, g_ref, wu_ref, bu_ref, wg_ref, bg_ref, skip_ref, kf_ref,
                 fa_ref, ga_ref, f2_ref, f2i_ref, o_ref, s_ref):
    row = lax.broadcasted_iota(jnp.int32, (FFT_NH, LANES), 0)

    def blk(ref, n2):
        return [ref[b, pl.ds(n2, FFT_NH, stride=FFT_N2), :] for b in range(2)]

    def down(x):
        return jnp.where(row == 0, 0.0, pltpu.roll(x, 1, axis=0))

    def up(x):
        return jnp.where(row == FFT_NH - 1, 0.0, pltpu.roll(x, FFT_NH - 1, axis=0))

    def cat(pair):
        return jnp.concatenate(pair, axis=0)

    def sconv(w_ref, b_ref, prev, cur, nxt):
        return prev * w_ref[0] + cur * w_ref[1] + nxt * w_ref[2] + b_ref[...]

    def first(ref):
        return cat([down(x) for x in blk(ref, FFT_N2 - 1)]), cat(blk(ref, 0))

    def last(ref):
        return cat([up(x) for x in blk(ref, 0)])

    def fwd_store(n2, ublk):
        s_ref[_strided(n2), :] = jnp.dot(fa_ref[n2], ublk.astype(BF16), preferred_element_type=F32)

    if sc_u:
        def fwd_body(n2, carry):
            prev, cur = carry
            nxt = cat(blk(u_ref, n2 + 1))
            fwd_store(n2, sconv(wu_ref, bu_ref, prev, cur, nxt))
            return cur, nxt

        prev, cur = lax.fori_loop(0, FFT_N2 - 1, fwd_body, first(u_ref), unroll=DFT_UNROLL)
        fwd_store(FFT_N2 - 1, sconv(wu_ref, bu_ref, prev, cur, last(u_ref)))
    else:
        def fwd_body(n2, carry):
            fwd_store(n2, cat(blk(u_ref, n2)))
            return carry

        lax.fori_loop(0, FFT_N2, fwd_body, 0, unroll=DFT_UNROLL)

    def mid_body(k1, carry):
        r0 = pl.multiple_of(k1 * S_PITCH, 8)
        r1 = pl.multiple_of((FFT_N1 + k1) * S_PITCH, 8)
        rhs = jnp.concatenate([s_ref[pl.ds(r0, FFT_N2), :], s_ref[pl.ds(r1, FFT_N2), :]], axis=0).astype(BF16)
        x = jnp.dot(f2_ref[...], rhs, preferred_element_type=F32)
        kf = kf_ref[0, k1].astype(F32)
        xr, xi = x[:FFT_N2], x[FFT_N2:]
        kr, ki = kf[:FFT_N2], kf[FFT_N2:]
        y = jnp.concatenate([xr * kr - xi * ki, xr * ki + xi * kr], axis=0).astype(BF16)
        p = jnp.dot(f2i_ref[...], y, preferred_element_type=F32)
        s_ref[pl.ds(r0, FFT_N2), :] = p[:FFT_N2]
        s_ref[pl.ds(r1, FFT_N2), :] = p[FFT_N2:]
        return carry

    lax.fori_loop(0, FFT_N1, mid_body, 0, unroll=DFT_UNROLL)

    skip = skip_ref[0]

    def out_store(n2, ublk, gblk):
        c = jnp.dot(ga_ref[n2], s_ref[_strided(n2), :].astype(BF16), preferred_element_type=F32)
        y = gblk * (c + skip * ublk)
        o_ref[0, pl.ds(n2, FFT_NH, stride=FFT_N2), :] = y[:FFT_NH]
        o_ref[1, pl.ds(n2, FFT_NH, stride=FFT_N2), :] = y[FFT_NH:]

    if sc_u:
        def out_body(n2, carry):
            pu, cu, pg, cg = carry
            nu = cat(blk(u_ref, n2 + 1))
            ng = cat(blk(g_ref, n2 + 1))
            out_store(n2, sconv(wu_ref, bu_ref, pu, cu, nu), sconv(wg_ref, bg_ref, pg, cg, ng))
            return cu, nu, cg, ng

        pu, cu, pg, cg = lax.fori_loop(0, FFT_N2 - 1, out_body, first(u_ref) + first(g_ref),
                                       unroll=DFT_UNROLL)
        out_store(FFT_N2 - 1, sconv(wu_ref, bu_ref, pu, cu, last(u_ref)),
                  sconv(wg_ref, bg_ref, pg, cg, last(g_ref)))
    else:
        def out_body(n2, carry):
            pg, cg = carry
            ng = cat(blk(g_ref, n2 + 1))
            out_store(n2, cat(blk(u_ref, n2)), sconv(wg_ref, bg_ref, pg, cg, ng))
            return cg, ng

        pg, cg = lax.fori_loop(0, FFT_N2 - 1, out_body, first(g_ref), unroll=DFT_UNROLL)
        out_store(FFT_N2 - 1, cat(blk(u_ref, FFT_N2 - 1)), sconv(wg_ref, bg_ref, pg, cg, last(g_ref)))


def _longconv(u, u_part, gate, g_part, conv_w, conv_b, skip, kf, order, sc_u):
    B = u.shape[0]
    nct = HY_DIM // LANES
    fa, ga, f2b, f2ib, _ = _dft_tables()
    fa, ga, f2b, f2ib = (_table_bf16(t) for t in (fa, ga, f2b, f2ib))
    w3 = conv_w.reshape(3, 1, HY_COLS)
    b3 = conv_b.reshape(1, HY_COLS)
    skip2 = skip.reshape(HY_ORDER, 1, HY_DIM)
    full = lambda a: pl.BlockSpec(a.shape, lambda c, p: (0,) * a.ndim)
    sig = lambda part: pl.BlockSpec((2, SEQ, LANES), lambda c, p: (p, 0, part * nct + c))
    wspec = lambda part: pl.BlockSpec((3, 1, LANES), lambda c, p: (0, 0, part * nct + c))
    bspec = lambda part: pl.BlockSpec((1, LANES), lambda c, p: (0, part * nct + c))
    wu_part = u_part if sc_u else g_part
    return pl.pallas_call(
        functools.partial(_conv_kernel, sc_u),
        out_shape=jax.ShapeDtypeStruct((B, SEQ, HY_DIM), F32),
        grid=(nct, B // 2),
        in_specs=[sig(u_part), sig(g_part), wspec(wu_part), bspec(wu_part), wspec(g_part), bspec(g_part),
                  pl.BlockSpec((1, 1, LANES), lambda c, p: (order, 0, c)),
                  pl.BlockSpec((1, FFT_N1, 2 * FFT_N2, LANES), lambda c, p: (order, 0, 0, c)),
                  full(fa), full(ga), full(f2b), full(f2ib)],
        out_specs=pl.BlockSpec((2, SEQ, LANES), lambda c, p: (p, 0, c)),
        scratch_shapes=[pltpu.VMEM((2 * FFT_N1 * S_PITCH, LANES), F32)],
        compiler_params=_cp(("arbitrary", "arbitrary")),
        name="conv%d" % order,
    )(u, gate, w3, b3, w3, b3, skip2, kf, fa, ga, f2b, f2ib)


def _merge_kernel(x_ref, at_ref, hy_ref, gt_ref, g1_ref, sh_ref, sc_ref, ng_ref, wa_ref, wh_ref, wo_ref,
                  x1_ref, h2_ref):
    ga = jax.nn.sigmoid(gt_ref[0, :, :D_MODEL].astype(F32))
    gh = jax.nn.sigmoid(gt_ref[0, :, D_MODEL:].astype(F32))
    merged = (ga * jnp.dot(at_ref[0], wa_ref[...], preferred_element_type=F32)
              + gh * jnp.dot(hy_ref[0].astype(BF16), wh_ref[...], preferred_element_type=F32))
    out = jnp.dot(merged.astype(BF16), wo_ref[...], preferred_element_type=F32)
    x1 = x_ref[0] + g1_ref[0] * out
    x1_ref[0] = x1
    h2_ref[0] = _norm_mod(x1, ng_ref[...], sh_ref[0], sc_ref[0]).astype(BF16)


def _merge(x, attn, hy, gates, g1, sh2, sc2, ng, wa, wh, wo, tm=512):
    B, L, D = x.shape
    full = lambda a: pl.BlockSpec(a.shape, lambda b, i: (0,) * a.ndim)
    row = pl.BlockSpec((1, 1, D), lambda b, i: (b, 0, 0))
    tok = lambda w: pl.BlockSpec((1, tm, w), lambda b, i: (b, i, 0))
    return pl.pallas_call(
        _merge_kernel,
        out_shape=(jax.ShapeDtypeStruct((B, L, D), F32), jax.ShapeDtypeStruct((B, L, D), BF16)),
        grid=(B, L // tm),
        in_specs=[tok(D), tok(Q_DIM), tok(HY_DIM), tok(GATE_COLS), row, row, row, full(ng),
                  full(wa), full(wh), full(wo)],
        out_specs=(tok(D), tok(D)),
        compiler_params=_cp(("arbitrary", "arbitrary")),
        name="merge",
    )(x, attn, hy, gates, g1, sh2, sc2, ng, wa, wh, wo)


def _top16_rows(s, pos, sentinel, idx=None):
    vals, pays = [], []
    for _ in range(PEER_TOPK):
        m = jnp.max(s, axis=0, keepdims=True)
        first = jnp.min(jnp.where(s == m, pos, sentinel), axis=0, keepdims=True)
        hit = pos == first
        pays.append(first if idx is None else jnp.sum(jnp.where(hit, idx, 0.0), axis=0, keepdims=True))
        vals.append(m)
        s = jnp.where(hit, -jnp.inf, s)
    return jnp.concatenate(vals, axis=0), jnp.concatenate(pays, axis=0)


_CAND_GROUPS = [(0, 0, 8), (0, 8, 8), (1, 0, 8), (2, 0, 5), (3, 0, 4), (4, 0, 3), (5, 0, 2), (6, 0, 2), (7, 0, 2)]


def _pair_candidates(v1, i1, v2, i2):
    tt = v1.shape[1]
    r8 = lax.broadcasted_iota(jnp.int32, (8, tt), 0)
    r8f = r8.astype(F32)
    cand, pos, cidx = [], [], []
    for k1, k2, valid in _CAND_GROUPS:
        c = v1[k1:k1 + 1] + v2[k2:k2 + 8]
        cand.append(c if valid == 8 else jnp.where(r8 < valid, c, -jnp.inf))
        pos.append(r8f + float(k1 * PEER_TOPK + k2))
        cidx.append(i1[k1:k1 + 1] * float(PEER_N_KEYS) + i2[k2:k2 + 8])
    cand.append(v1[8:16] + v2[0:1])
    pos.append((r8f + 8.0) * float(PEER_TOPK))
    cidx.append(i1[8:16] * float(PEER_N_KEYS) + i2[0:1])
    return jnp.concatenate(cand, axis=0), jnp.concatenate(pos, axis=0), jnp.concatenate(cidx, axis=0)


def _select_head(h, xq, k1_ref, k2_ref, e_ref, g_ref):
    tt = xq.shape[1]
    key_id = lax.broadcasted_iota(jnp.int32, (PEER_N_KEYS, tt), 0).astype(F32)
    half = PEER_DK // 2
    q1 = xq[h * PEER_DK:h * PEER_DK + half].astype(BF16)
    q2 = xq[h * PEER_DK + half:(h + 1) * PEER_DK].astype(BF16)
    s1 = jnp.dot(k1_ref[h], q1, preferred_element_type=F32)
    s2 = jnp.dot(k2_ref[h], q2, preferred_element_type=F32)
    v1, i1 = _top16_rows(s1, key_id, float(PEER_N_KEYS))
    v2, i2 = _top16_rows(s2, key_id, float(PEER_N_KEYS))
    cand, pos, cidx = _pair_candidates(v1, i1, v2, i2)
    best, eidx = _top16_rows(cand, pos, float(PEER_TOPK * PEER_TOPK), cidx)
    p = jnp.exp(best - best[0:1])
    g_ref[h * PEER_TOPK:(h + 1) * PEER_TOPK, :] = p / jnp.sum(p, axis=0, keepdims=True)
    e_ref[h * PEER_TOPK:(h + 1) * PEER_TOPK, :] = eidx


GATE_SUB = 32


def _gate_rows(e_all, g_all, c0, s_ref):
    nk = PEER_N_KEYS
    pitch = s_ref.shape[0] // nk
    ids = lax.broadcasted_iota(jnp.int32, (GATE_SUB, nk, nk), 1).astype(F32)
    e = e_all[c0:c0 + GATE_SUB, :]
    a = jnp.floor(e * (1.0 / nk))
    b = e - a * nk
    g = g_all[c0:c0 + GATE_SUB, :]
    at = jnp.where(ids == a[:, None, :], 1.0, 0.0).astype(BF16)
    bt = jnp.where(ids == b[:, None, :], g[:, None, :], 0.0).astype(BF16)
    g3 = jnp.einsum("caj,cbj->cab", at, bt, preferred_element_type=F32)
    for c in range(GATE_SUB):
        s_ref[pl.ds(c0 + c, nk, stride=pitch), :] = g3[c]


def _gate_out(a0, rows, tm, o_ref, s_ref):
    pitch = s_ref.shape[0] // PEER_N_KEYS
    for a in range(a0, a0 + rows):
        o_ref[a] = s_ref[a * pitch:a * pitch + tm, :].astype(BF16)


def _select_kernel(h_ref, wq_ref, k1_ref, k2_ref, o_ref, eg_ref, s_ref):
    i = pl.program_id(0)
    slot = i % 2
    tt = h_ref.shape[0]

    @pl.when(i == 0)
    def _():
        eg_ref[1] = jnp.zeros(eg_ref.shape[1:], F32)

    prev = eg_ref.at[1 - slot]
    e_prev, g_prev = prev[0].T, prev[1].T
    cur = eg_ref.at[slot]
    nt = (((1,), (1,)), ((), ()))
    xq = lax.dot_general(wq_ref[...], h_ref[...], nt, preferred_element_type=F32)
    build_heads = PEER_HEADS // 2
    subs = tt // GATE_SUB // build_heads
    out_rows = PEER_N_KEYS // (PEER_HEADS - build_heads)
    for h in range(PEER_HEADS):
        if h < build_heads:
            for k in range(subs):
                _gate_rows(e_prev, g_prev, (h * subs + k) * GATE_SUB, s_ref)
        else:
            _gate_out((h - build_heads) * out_rows, out_rows, tt, o_ref, s_ref)
        _select_head(h, xq, k1_ref, k2_ref, cur.at[0], cur.at[1])


def _peer_select(h2, wq_t, keys1, keys2, tt=256):
    n = h2.shape[0]
    nt = n // tt
    full = lambda a: pl.BlockSpec(a.shape, lambda i: (0,) * a.ndim)
    return pl.pallas_call(
        _select_kernel,
        out_shape=jax.ShapeDtypeStruct((PEER_N_KEYS, n, PEER_N_KEYS), BF16),
        grid=(nt + 1,),
        in_specs=[pl.BlockSpec((tt, D_MODEL), lambda i: (jnp.minimum(i, nt - 1), 0)),
                  full(wq_t), full(keys1), full(keys2)],
        out_specs=pl.BlockSpec((PEER_N_KEYS, tt, PEER_N_KEYS), lambda i: (0, jnp.maximum(i - 1, 0), 0)),
        scratch_shapes=[pltpu.VMEM((2, 2, PEER_HEADS * PEER_TOPK, tt), F32),
                        pltpu.VMEM((PEER_N_KEYS * (tt + 8), LANES), F32)],
        compiler_params=_cp(("arbitrary",)),
        name="select",
    )(h2, wq_t, keys1, keys2)


def _dense_kernel(h_ref, ut_ref, v_ref, gm_ref, x1_ref, g2_ref, fg_ref, o_ref, acc_ref):
    j = pl.program_id(1)

    @pl.when(j == 0)
    def _():
        acc_ref[...] = jnp.zeros_like(acc_ref)

    a = jnp.dot(h_ref[...], ut_ref[...], preferred_element_type=F32)
    na = gm_ref.shape[0]
    w = jnp.concatenate(
        [jax.nn.gelu(a[:, i * LANES:(i + 1) * LANES], approximate=True) * gm_ref[i].astype(F32)
         for i in range(na)], axis=1).astype(BF16)
    acc_ref[...] += jnp.dot(w, v_ref[...], preferred_element_type=F32)

    @pl.when(j == pl.num_programs(1) - 1)
    def _():
        x2 = x1_ref[...] + g2_ref[0] * acc_ref[...]
        ms = jnp.mean(x2 * x2, axis=-1, keepdims=True)
        o_ref[...] = x2 * lax.rsqrt(ms + EPS) * fg_ref[...]


def _peer_dense(h2, u_t, v, gm, x1, g2, fg, seq, tm=512, ne=2048):
    n, d = h2.shape
    n_exp = v.shape[0]
    per_b = seq // tm
    return pl.pallas_call(
        _dense_kernel,
        out_shape=jax.ShapeDtypeStruct((n, d), F32),
        grid=(n // tm, n_exp // ne),
        in_specs=[pl.BlockSpec((tm, d), lambda i, j: (i, 0)),
                  pl.BlockSpec((d, ne), lambda i, j: (0, j)),
                  pl.BlockSpec((ne, d), lambda i, j: (j, 0)),
                  pl.BlockSpec((ne // LANES, tm, LANES), lambda i, j: (j, i, 0)),
                  pl.BlockSpec((tm, d), lambda i, j: (i, 0)),
                  pl.BlockSpec((1, 1, d), lambda i, j: (i // per_b, 0, 0)),
                  pl.BlockSpec((1, d), lambda i, j: (0, 0))],
        out_specs=pl.BlockSpec((tm, d), lambda i, j: (i, 0)),
        scratch_shapes=[pltpu.VMEM((tm, d), F32)],
        compiler_params=_cp(("arbitrary", "arbitrary")),
        name="dense",
    )(h2, u_t, v, gm, x1, g2, fg)


def kernel(x, c, ctx, c_ctx, ada_w, ada_b, norm_mix_g, norm_ffn_g, w_in, q_norm_g, k_norm_g, hy_conv_w, hy_conv_b, hf_w1, hf_b1, hf_w2, hf_b2, hf_w3, hf_b3, hf_w4, hf_freq, hy_skip, w_attn_out, w_hy_out, w_out, peer_wq, peer_keys1, peer_keys2, peer_u, peer_v, final_norm_g):
    B, L, D = x.shape
    assert L == SEQ and D == D_MODEL and B % 2 == 0 and ada_w.shape[0] == 1
    bf = lambda a: a.astype(BF16)

    c16 = jnp.zeros((16, D), F32).at[:B].set(c).at[B].set(c_ctx)
    mod = _modulation(c16, ada_w[0], ada_b[0])
    part = lambda i: mod[:B, i * D:(i + 1) * D].reshape(B, 1, D)
    sh1, sc1, g1, sh2, sc2, g2 = (part(i) for i in range(6))
    csh1, csc1 = mod[B:B + 1, 0:D], mod[B:B + 1, D:2 * D]

    w = w_in[0]
    o_k, o_v, o_h, o_g = Q_DIM, Q_DIM + KV_DIM, Q_DIM + 2 * KV_DIM, Q_DIM + 2 * KV_DIM + HY_COLS
    wq, wk, wv, wh, wg = bf(w[:, :o_k]), bf(w[:, o_k:o_v]), bf(w[:, o_v:o_h]), bf(w[:, o_h:o_g]), bf(w[:, o_g:])
    gq = jnp.tile(q_norm_g[0], N_HEADS).reshape(1, Q_DIM)
    gk = jnp.tile(k_norm_g[0], N_KV_HEADS).reshape(1, KV_DIM)
    ng1 = norm_mix_g[0].reshape(1, D)
    cos, sin = (jnp.asarray(t) for t in _rope_tables())

    q, ks, vs, zhy, gates = _inproj(x, sh1, sc1, ng1, wq, wk, wv, wh, wg, gq, gk, cos, sin)
    kc, vc = _ctxkv(ctx, csh1, csc1, ng1, wk, wv, gk)
    attn = _attention(q, ks, kc, vs, vc)

    kf = _filters(hf_w1[0], hf_b1[0], hf_w2[0], hf_b2[0], hf_w3[0], hf_b3[0], hf_w4[0], hf_freq[0])
    y1 = _longconv(zhy, 0, zhy, 1, hy_conv_w[0], hy_conv_b[0], hy_skip[0], kf, 0, True)
    y2 = _longconv(y1, 0, zhy, 2, hy_conv_w[0], hy_conv_b[0], hy_skip[0], kf, 1, False)

    x1, h2 = _merge(x, attn, y2, gates, g1, sh2, sc2, norm_ffn_g[0].reshape(1, D),
                    bf(w_attn_out[0]), bf(w_hy_out[0]), bf(w_out[0]))

    n = B * L
    h2f = h2.reshape(n, D)
    gm = _peer_select(h2f, bf(peer_wq[0].T), bf(peer_keys1[0]), bf(peer_keys2[0]))
    out = _peer_dense(h2f, bf(peer_u[0].T), bf(peer_v[0]), gm, x1.reshape(n, D), g2,
                      final_norm_g.reshape(1, D), L)
    return out.reshape(B, L, D)
```
